```python
import math
import jax, jax.numpy as jnp
from jax import lax
import numpy as np

D_MODEL = 1024
BATCH = 8
SEQ = 4096
DEPTH = 1

N_MEM = 256
D_FF = 2816
D_POOL = 512
POOL_WINDOWS = (2, 4, 8, 16)
N_POOL_GROUPS = len(POOL_WINDOWS)
POOL_GROUP = D_POOL // N_POOL_GROUPS
D_SSM = 256
SSM_GROUP = 16
N_SSM_GROUPS = D_SSM // SSM_GROUP
SSM_STATE = 64
N_XHEADS = 4
XHEAD_DIM = D_MODEL // N_XHEADS
D_IN = D_POOL + D_SSM + 2 * D_MODEL
EPS = 1e-6

kernel_name = "hybrid_pool_s5_gated_encoder_layer"


def _rms_norm(v, g):
    vf = v.astype(jnp.float32)
    r = lax.rsqrt(jnp.mean(vf * vf, axis=-1, keepdims=True) + EPS)
    return (vf * r).astype(v.dtype) * g


def _swiglu(u, w_gate, w_up, w_down):
    return (jax.nn.silu(u @ w_gate) * (u @ w_up)) @ w_down


def _centred_pool_minus_self(v, window):
    L = v.shape[1]
    left = window // 2
    right = window - 1 - left
    c = jnp.concatenate([jnp.zeros_like(v[:, :1]), jnp.cumsum(v, axis=1)], axis=1)
    t = jnp.arange(L)
    lo = jnp.clip(t - left, 0, L)
    hi = jnp.clip(t + right + 1, 0, L)
    s = jnp.take(c, hi, axis=1) - jnp.take(c, lo, axis=1)
    cnt = (hi - lo).astype(jnp.float32)[None, :, None]
    return s / cnt - v


def _pool_mixer(p, pool_w, pool_scale):
    B_, L, _ = p.shape
    pf = p.astype(jnp.float32)
    groups = [
        _centred_pool_minus_self(pf[..., g * POOL_GROUP:(g + 1) * POOL_GROUP], w)
        for g, w in enumerate(POOL_WINDOWS)
    ]
    pooled = jnp.stack(groups, axis=2).astype(p.dtype)
    mixed = jnp.einsum('blgc,gcd->blgd', pooled, pool_w)
    return mixed.reshape(B_, L, D_POOL) * pool_scale


def _complex_linear_combine(e1, e2):
    a1r, a1i, b1r, b1i = e1
    a2r, a2i, b2r, b2i = e2
    ar = a2r * a1r - a2i * a1i
    ai = a2r * a1i + a2i * a1r
    br = a2r * b1r - a2i * b1i + b2r
    bi = a2r * b1i + a2i * b1r + b2i
    return (ar, ai, br, bi)


def _s5_bidirectional(s, a_re, a_im, log_dt, b_re, b_im, c_re, c_im, d_skip):
    B_, L, _ = s.shape
    uf = s.astype(jnp.float32).reshape(B_, L, N_SSM_GROUPS, SSM_GROUP)
    y = uf * d_skip.astype(jnp.float32).reshape(N_SSM_GROUPS, SSM_GROUP)
    for direction in range(2):
        ar = a_re[direction].astype(jnp.float32)
        ai = a_im[direction].astype(jnp.float32)
        dt = jnp.exp(log_dt[direction].astype(jnp.float32))[:, None]
        mag = jnp.exp(dt * ar)
        ang = dt * ai
        abr = mag * jnp.cos(ang)
        abi = mag * jnp.sin(ang)
        den = ar * ar + ai * ai
        nr = abr - 1.0
        qr = (nr * ar + abi * ai) / den
        qi = (abi * ar - nr * ai) / den
        br = b_re[direction].astype(jnp.float32)
        bi = b_im[direction].astype(jnp.float32)
        bbr = qr[..., None] * br - qi[..., None] * bi
        bbi = qr[..., None] * bi + qi[..., None] * br
        bur = jnp.einsum('gph,blgh->blgp', bbr, uf)
        bui = jnp.einsum('gph,blgh->blgp', bbi, uf)
        a_r = jnp.broadcast_to(abr, bur.shape)
        a_i = jnp.broadcast_to(abi, bur.shape)
        _, _, xr, xi = lax.associative_scan(
            _complex_linear_combine, (a_r, a_i, bur, bui), axis=1, reverse=(direction == 1))
        cr = c_re[direction].astype(jnp.float32)
        ci = c_im[direction].astype(jnp.float32)
        y = y + jnp.einsum('ghp,blgp->blgh', cr, xr) - jnp.einsum('ghp,blgp->blgh', ci, xi)
    return y.reshape(B_, L, D_SSM).astype(s.dtype)


def _cross_attention(u, mem_n, w_q, w_kv, w_xo):
    B_, L, _ = u.shape
    M = mem_n.shape[1]
    q = (u @ w_q).reshape(B_, L, N_XHEADS, XHEAD_DIM)
    kv = mem_n @ w_kv
    k = kv[..., :D_MODEL].reshape(B_, M, N_XHEADS, XHEAD_DIM)
    v = kv[..., D_MODEL:].reshape(B_, M, N_XHEADS, XHEAD_DIM)
    scores = jnp.einsum('blhd,bmhd->bhlm', q.astype(jnp.float32), k.astype(jnp.float32)) / math.sqrt(XHEAD_DIM)
    probs = jax.nn.softmax(scores, axis=-1).astype(u.dtype)
    o = jnp.einsum('bhlm,bmhd->blhd', probs, v).reshape(B_, L, D_MODEL)
    return o @ w_xo


def setup_inputs(seed: int = 0) -> dict:
    key = jax.random.key(seed)
    ks = iter(jax.random.split(key, 48))
    f32 = jnp.float32

    def nrm(shape, scale):
        return jax.random.normal(next(ks), shape, f32) * scale

    def gain(shape):
        return 1.0 + 0.02 * jax.random.normal(next(ks), shape, f32)

    L_ = DEPTH
    G, P, H = N_SSM_GROUPS, SSM_STATE, SSM_GROUP
    inp = {}
    inp['x'] = jax.random.normal(next(ks), (BATCH, SEQ, D_MODEL), f32)
    inp['mem'] = jax.random.normal(next(ks), (BATCH, N_MEM, D_MODEL), f32)
    inp['ffn1_norm'] = gain((L_, D_MODEL))
    inp['ffn1_w_gate'] = nrm((L_, D_MODEL, D_FF), D_MODEL ** -0.5)
    inp['ffn1_w_up'] = nrm((L_, D_MODEL, D_FF), D_MODEL ** -0.5)
    inp['ffn1_w_down'] = nrm((L_, D_FF, D_MODEL), D_FF ** -0.5)
    inp['mix_norm'] = gain((L_, D_MODEL))
    inp['w_in'] = nrm((L_, D_MODEL, D_IN), D_MODEL ** -0.5)
    inp['pool_w'] = nrm((L_, N_POOL_GROUPS, POOL_GROUP, POOL_GROUP), POOL_GROUP ** -0.5)
    inp['pool_scale'] = gain((L_, D_POOL))
    inp['w_pool_proj'] = nrm((L_, D_POOL, D_MODEL), D_POOL ** -0.5)
    a_re = -0.5 + 0.01 * jax.random.normal(next(ks), (L_, 2, G, P), f32)
    a_im = math.pi * jnp.arange(P, dtype=f32) + 0.01 * jax.random.normal(next(ks), (L_, 2, G, P), f32)
    inp['ssm_a_re'] = a_re
    inp['ssm_a_im'] = a_im
    inp['ssm_log_dt'] = jax.random.uniform(next(ks), (L_, 2, G), f32, math.log(1e-3), math.log(1e-1))
    inp['ssm_b_re'] = nrm((L_, 2, G, P, H), (2.0 * H) ** -0.5)
    inp['ssm_b_im'] = nrm((L_, 2, G, P, H), (2.0 * H) ** -0.5)
    inp['ssm_c_re'] = nrm((L_, 2, G, H, P), (2.0 * P) ** -0.5)
    inp['ssm_c_im'] = nrm((L_, 2, G, H, P), (2.0 * P) ** -0.5)
    inp['ssm_d'] = nrm((L_, D_SSM), 1.0)
    inp['w_glu_val'] = nrm((L_, D_SSM, D_MODEL), D_SSM ** -0.5)
    inp['w_glu_gate'] = nrm((L_, D_SSM, D_MODEL), D_SSM ** -0.5)
    inp['w_mix_out'] = nrm((L_, D_MODEL, D_MODEL), D_MODEL ** -0.5)
    inp['xattn_norm'] = gain((L_, D_MODEL))
    inp['mem_norm'] = gain((L_, D_MODEL))
    inp['w_q'] = nrm((L_, D_MODEL, D_MODEL), D_MODEL ** -0.5)
    inp['w_kv'] = nrm((L_, D_MODEL, 2 * D_MODEL), D_MODEL ** -0.5)
    inp['w_xo'] = nrm((L_, D_MODEL, D_MODEL), D_MODEL ** -0.5)
    inp['ffn2_norm'] = gain((L_, D_MODEL))
    inp['ffn2_w_gate'] = nrm((L_, D_MODEL, D_FF), D_MODEL ** -0.5)
    inp['ffn2_w_up'] = nrm((L_, D_MODEL, D_FF), D_MODEL ** -0.5)
    inp['ffn2_w_down'] = nrm((L_, D_FF, D_MODEL), D_FF ** -0.5)
    inp['final_norm'] = gain((D_MODEL,))
    return inp


def reference(x, mem, ffn1_norm, ffn1_w_gate, ffn1_w_up, ffn1_w_down,
              mix_norm, w_in, pool_w, pool_scale, w_pool_proj,
              ssm_a_re, ssm_a_im, ssm_log_dt, ssm_b_re, ssm_b_im, ssm_c_re, ssm_c_im, ssm_d,
              w_glu_val, w_glu_gate, w_mix_out,
              xattn_norm, mem_norm, w_q, w_kv, w_xo,
              ffn2_norm, ffn2_w_gate, ffn2_w_up, ffn2_w_down, final_norm):
    h = x
    for l in range(DEPTH):
        h = h + 0.5 * _swiglu(_rms_norm(h, ffn1_norm[l]), ffn1_w_gate[l], ffn1_w_up[l], ffn1_w_down[l])

        u = _rms_norm(h, mix_norm[l])
        proj = u @ w_in[l]
        p = proj[..., :D_POOL]
        s = proj[..., D_POOL:D_POOL + D_SSM]
        g_pool = proj[..., D_POOL + D_SSM:D_POOL + D_SSM + D_MODEL]
        g_ssm = proj[..., D_POOL + D_SSM + D_MODEL:]

        z_pool = _pool_mixer(p, pool_w[l], pool_scale[l]) @ w_pool_proj[l]
        y_ssm = jax.nn.gelu(_s5_bidirectional(s, ssm_a_re[l], ssm_a_im[l], ssm_log_dt[l],
                                              ssm_b_re[l], ssm_b_im[l], ssm_c_re[l], ssm_c_im[l], ssm_d[l]))
        z_ssm = (y_ssm @ w_glu_val[l]) * jax.nn.sigmoid(y_ssm @ w_glu_gate[l])

        merged = jax.nn.sigmoid(g_pool) * z_pool + jax.nn.sigmoid(g_ssm) * z_ssm
        h = h + merged @ w_mix_out[l]

        h = h + _cross_attention(_rms_norm(h, xattn_norm[l]), _rms_norm(mem, mem_norm[l]),
                                 w_q[l], w_kv[l], w_xo[l])

        h = h + 0.5 * _swiglu(_rms_norm(h, ffn2_norm[l]), ffn2_w_gate[l], ffn2_w_up[l], ffn2_w_down[l])
    return _rms_norm(h, final_norm)
```

```python
import functools
import math

import jax
import jax.numpy as jnp
from jax import lax
from jax.experimental import pallas as pl
from jax.experimental.pallas import tpu as pltpu

F32 = jnp.float32
BF16 = jnp.bfloat16

D_MODEL = 1024
N_MEM = 256
D_FF = 2816
D_POOL = 512
POOL_WINDOWS = (2, 4, 8, 16)
POOL_GROUP = 128
D_SSM = 256
SSM_GROUP = 16
N_SSM_GROUPS = 16
SSM_STATE = 64
D_STATE = N_SSM_GROUPS * SSM_STATE
N_XHEADS = 4
XHEAD_DIM = 256
EPS = 1e-6

V7X_VMEM_LIMIT_BYTES = 56 * 1024 * 1024
POOL_HALO = 16

FFN_TILE = 512
MIX_TILE = 256
SSM_TILE = 64
FF_CHUNKS = ((0, 768), (768, 1536), (1536, 2304), (2304, 2816))
SCAN_LANES = 512
SCAN_UNROLL = 8


def _rms(v, g):
    r = lax.rsqrt(jnp.mean(v * v, axis=-1, keepdims=True) + EPS)
    return (v * r) * g


def _dot(a, b):
    return jnp.dot(a, b, preferred_element_type=F32)


def _const_spec(shape):
    nd = len(shape)
    return pl.BlockSpec(shape, lambda *_: (0,) * nd)


def _kv_kernel(mem_ref, g_ref, wkv_ref, k_ref, v_ref):
    m = _rms(mem_ref[...], g_ref[...]).astype(BF16)
    kv = _dot(m, wkv_ref[...])
    k_ref[...] = kv[:, :D_MODEL].astype(BF16)
    v_ref[...] = kv[:, D_MODEL:].astype(BF16)


def _kv_call(mem, mem_norm, w_kv):
    B = mem.shape[0]
    return pl.pallas_call(
        _kv_kernel,
        grid=(B,),
        in_specs=[
            pl.BlockSpec((None, N_MEM, D_MODEL), lambda b: (b, 0, 0)),
            _const_spec((1, D_MODEL)),
            _const_spec((D_MODEL, 2 * D_MODEL)),
        ],
        out_specs=[
            pl.BlockSpec((None, N_MEM, D_MODEL), lambda b: (b, 0, 0)),
            pl.BlockSpec((None, N_MEM, D_MODEL), lambda b: (b, 0, 0)),
        ],
        out_shape=[jax.ShapeDtypeStruct((B, N_MEM, D_MODEL), BF16)] * 2,
        compiler_params=pltpu.CompilerParams(
            dimension_semantics=("arbitrary",), vmem_limit_bytes=V7X_VMEM_LIMIT_BYTES),
        name="kv_proj",
    )(mem, mem_norm, w_kv)


def _swiglu_residual(x, g, wg_ref, wu_ref, wd_ref):
    u = _rms(x, g).astype(BF16)
    acc = None
    for c0, c1 in FF_CHUNKS:
        gate = _dot(u, wg_ref[:, c0:c1])
        up = _dot(u, wu_ref[:, c0:c1])
        act = (gate * jax.nn.sigmoid(gate) * up).astype(BF16)
        part = _dot(act, wd_ref[c0:c1, :])
        acc = part if acc is None else acc + part
    return x + 0.5 * acc


def _ffn1_kernel(x_ref, n1_ref, wg_ref, wu_ref, wd_ref, n2_ref, wps_ref, h_ref, p_ref, s_ref):
    h = _swiglu_residual(x_ref[...], n1_ref[...], wg_ref, wu_ref, wd_ref)
    h_ref[...] = h
    u = _rms(h, n2_ref[...]).astype(BF16)
    ps = _dot(u, wps_ref[...])
    p_ref[...] = ps[:, :D_POOL]
    s_ref[...] = ps[:, D_POOL:]


def _ffn1_call(x, n1, wg, wu, wd, n2, wps):
    B, L, D = x.shape
    T = FFN_TILE
    tok = pl.BlockSpec((None, T, D), lambda b, i: (b, i, 0))
    return pl.pallas_call(
        _ffn1_kernel,
        grid=(B, L // T),
        in_specs=[
            tok,
            _const_spec((1, D)),
            _const_spec((D, D_FF)),
            _const_spec((D, D_FF)),
            _const_spec((D_FF, D)),
            _const_spec((1, D)),
            _const_spec((D, D_POOL + D_SSM)),
        ],
        out_specs=[
            tok,
            pl.BlockSpec((None, T, D_POOL), lambda b, i: (b, i, 0)),
            pl.BlockSpec((T, D_SSM), lambda b, i: (i, b)),
        ],
        out_shape=[
            jax.ShapeDtypeStruct((B, L, D), F32),
            jax.ShapeDtypeStruct((B, L, D_POOL), F32),
            jax.ShapeDtypeStruct((L, B * D_SSM), F32),
        ],
        compiler_params=pltpu.CompilerParams(
            dimension_semantics=("arbitrary", "arbitrary"), vmem_limit_bytes=V7X_VMEM_LIMIT_BYTES),
        name="ffn1",
    )(x, n1, wg, wu, wd, n2, wps)


def _ffn2_kernel(h_ref, n_ref, wg_ref, wu_ref, wd_ref, fn_ref, o_ref):
    h = _swiglu_residual(h_ref[...], n_ref[...], wg_ref, wu_ref, wd_ref)
    o_ref[...] = _rms(h, fn_ref[...])


def _ffn2_call(h, n, wg, wu, wd, fn):
    B, L, D = h.shape
    T = FFN_TILE
    tok = pl.BlockSpec((None, T, D), lambda b, i: (b, i, 0))
    return pl.pallas_call(
        _ffn2_kernel,
        grid=(B, L // T),
        in_specs=[
            tok,
            _const_spec((1, D)),
            _const_spec((D, D_FF)),
            _const_spec((D, D_FF)),
            _const_spec((D_FF, D)),
            _const_spec((1, D)),
        ],
        out_specs=tok,
        out_shape=jax.ShapeDtypeStruct((B, L, D), F32),
        compiler_params=pltpu.CompilerParams(
            dimension_semantics=("arbitrary", "arbitrary"), vmem_limit_bytes=V7X_VMEM_LIMIT_BYTES),
        name="ffn2",
    )(h, n, wg, wu, wd, fn)


def _ssm_kernel(sf_ref, sb_ref, are_ref, aim_ref, ldt_ref, bre_ref, bim_ref, crt_ref, cit_ref, d_ref,
                yf_ref, yb_ref, abar_scr, bmat_scr, state_scr, bu_scr):
    i = pl.program_id(0)
    S = D_STATE
    R = sf_ref.shape[0]
    steps = R // 8

    @pl.when(i == 0)
    def _():
        for d in range(2):
            ar = are_ref[d]
            ai = aim_ref[d]
            dt = jnp.exp(ldt_ref[d])
            mag = jnp.exp(dt * ar)
            ang = dt * ai
            abr = mag * jnp.cos(ang)
            abi = mag * jnp.sin(ang)
            den = ar * ar + ai * ai
            nr = abr - 1.0
            qr = (nr * ar + abi * ai) / den
            qi = (abi * ar - nr * ai) / den
            abar_scr[d, 0] = jnp.broadcast_to(abr, (8, S))
            abar_scr[d, 1] = jnp.broadcast_to(abi, (8, S))
            br = bre_ref[d]
            bi = bim_ref[d]
            bmat_scr[d, :, :S] = (qr * br - qi * bi).astype(BF16)
            bmat_scr[d, :, S:] = (qr * bi + qi * br).astype(BF16)
        state_scr[...] = jnp.zeros_like(state_scr)

    for d, s_ref, y_ref in ((0, sf_ref, yf_ref), (1, sb_ref, yb_ref)):
        s = s_ref[...]
        bu_scr[d] = _dot(s.astype(BF16), bmat_scr[d])

        for c0 in range(0, S, SCAN_LANES):
            re = pl.ds(c0, SCAN_LANES)
            im = pl.ds(S + c0, SCAN_LANES)
            ar = abar_scr[d, 0, :, re]
            ai = abar_scr[d, 1, :, re]

            def step(k, carry, d=d, re=re, im=im, ar=ar, ai=ai):
                xr, xi = carry
                t = k if d == 0 else steps - 1 - k
                rows = pl.ds(pl.multiple_of(t * 8, 8), 8)
                nxr = ar * xr - ai * xi + bu_scr[d, rows, re]
                nxi = ar * xi + ai * xr + bu_scr[d, rows, im]
                bu_scr[d, rows, re] = nxr
                bu_scr[d, rows, im] = nxi
                return nxr, nxi

            xr, xi = lax.fori_loop(0, steps, step, (state_scr[d, 0, :, re], state_scr[d, 1, :, re]),
                                   unroll=SCAN_UNROLL)
            state_scr[d, 0, :, re] = xr
            state_scr[d, 1, :, re] = xi

        xre = bu_scr[d, :, :S].astype(BF16)
        xim = bu_scr[d, :, S:].astype(BF16)
        y = _dot(xre, crt_ref[d]) - _dot(xim, cit_ref[d])
        if d == 0:
            y = y + s * d_ref[...]
        y_ref[...] = y


def _ssm_call(s_tm, a_re, a_im, log_dt, b_re_bd, b_im_bd, c_re_t, c_im_t, d_skip):
    R_total = s_tm.shape[0]
    R = SSM_TILE * 8
    n = R_total // R
    S = D_STATE
    fwd = pl.BlockSpec((R, D_SSM), lambda i: (i, 0))
    bwd = pl.BlockSpec((R, D_SSM), lambda i: (n - 1 - i, 0))
    return pl.pallas_call(
        _ssm_kernel,
        grid=(n,),
        in_specs=[
            fwd, bwd,
            _const_spec((2, 1, S)), _const_spec((2, 1, S)), _const_spec((2, 1, S)),
            _const_spec((2, D_SSM, S)), _const_spec((2, D_SSM, S)),
            _const_spec((2, S, D_SSM)), _const_spec((2, S, D_SSM)),
            _const_spec((1, D_SSM)),
        ],
        out_specs=[fwd, bwd],
        out_shape=[jax.ShapeDtypeStruct((R_total, D_SSM), F32)] * 2,
        scratch_shapes=[
            pltpu.VMEM((2, 2, 8, S), F32),
            pltpu.VMEM((2, D_SSM, 2 * S), BF16),
            pltpu.VMEM((2, 2, 8, S), F32),
            pltpu.VMEM((2, R, 2 * S), F32),
        ],
        compiler_params=pltpu.CompilerParams(
            dimension_semantics=("arbitrary",), vmem_limit_bytes=V7X_VMEM_LIMIT_BYTES),
        name="ssm_scan",
    )(s_tm, s_tm, a_re, a_im, log_dt, b_re_bd, b_im_bd, c_re_t, c_im_t, d_skip)


def _mix_kernel(h_ref, p_ref, pl_ref, pr_ref, yf_ref, yb_ref, k_ref, v_ref,
                nmix_ref, wgate_ref, poolw_ref, pscale_ref, wpp_ref, wglu_ref, wmo_ref,
                nx_ref, wq_ref, wxo_ref, o_ref, pext_scr, *, seq_len):
    i = pl.program_id(1)
    n_i = pl.num_programs(1)
    T = h_ref.shape[0]
    H = POOL_HALO
    h = h_ref[...]

    pext_scr[0:H, :] = jnp.where(i > 0, pl_ref[...], 0.0)
    pext_scr[H:H + T, :] = p_ref[...]
    pext_scr[H + T:, :] = jnp.where(i < n_i - 1, pr_ref[...], 0.0)
    t_glob = i * T + lax.broadcasted_iota(jnp.int32, (T, POOL_GROUP), 0)
    mixed = []
    for g, w in enumerate(POOL_WINDOWS):
        left = w // 2
        right = w - 1 - left
        lanes = slice(g * POOL_GROUP, (g + 1) * POOL_GROUP)
        tot = None
        for dlt in range(-left, right + 1):
            piece = pext_scr[H + dlt:H + dlt + T, lanes]
            tot = piece if tot is None else tot + piece
        hi = jnp.minimum(t_glob + (right + 1), seq_len)
        lo = jnp.maximum(t_glob - left, 0)
        cnt = (hi - lo).astype(F32)
        pooled = tot / cnt - pext_scr[H:H + T, lanes]
        mixed.append(_dot(pooled.astype(BF16), poolw_ref[g]))
    mixed = jnp.concatenate(mixed, axis=-1) * pscale_ref[...]
    z_pool = _dot(mixed.astype(BF16), wpp_ref[...])

    y = jax.nn.gelu(yf_ref[...] + yb_ref[...]).astype(BF16)
    glu = _dot(y, wglu_ref[...])
    z_ssm = glu[:, :D_MODEL] * jax.nn.sigmoid(glu[:, D_MODEL:])

    u = _rms(h, nmix_ref[...]).astype(BF16)
    gates = _dot(u, wgate_ref[...])
    merged = (jax.nn.sigmoid(gates[:, :D_MODEL]) * z_pool
              + jax.nn.sigmoid(gates[:, D_MODEL:]) * z_ssm)
    h = h + _dot(merged.astype(BF16), wmo_ref[...])

    q = _dot(_rms(h, nx_ref[...]).astype(BF16), wq_ref[...])
    heads = []
    for hd in range(N_XHEADS):
        cols = slice(hd * XHEAD_DIM, (hd + 1) * XHEAD_DIM)
        qh = q[:, cols].astype(BF16)
        sc = lax.dot_general(qh, k_ref[:, cols], (((1,), (1,)), ((), ())),
                             preferred_element_type=F32) / math.sqrt(XHEAD_DIM)
        e = jnp.exp(sc - jnp.max(sc, axis=-1, keepdims=True))
        probs = e / jnp.sum(e, axis=-1, keepdims=True)
        heads.append(_dot(probs.astype(BF16), v_ref[:, cols]))
    o = jnp.concatenate(heads, axis=-1).astype(BF16)
    o_ref[...] = h + _dot(o, wxo_ref[...])


def _mix_call(h, p, yf, yb, k, v, nmix, wgate, poolw, pscale, wpp, wglu, wmo, nx, wq, wxo):
    B, L, D = h.shape
    T = MIX_TILE
    H = POOL_HALO
    per_tile = T // H
    n_halo = L // H
    tok = pl.BlockSpec((None, T, D), lambda b, i: (b, i, 0))
    y_spec = pl.BlockSpec((T, D_SSM), lambda b, i: (i, b))
    mem_spec = pl.BlockSpec((None, N_MEM, D), lambda b, i: (b, 0, 0))
    return pl.pallas_call(
        functools.partial(_mix_kernel, seq_len=L),
        grid=(B, L // T),
        in_specs=[
            tok,
            pl.BlockSpec((None, T, D_POOL), lambda b, i: (b, i, 0)),
            pl.BlockSpec((None, H, D_POOL), lambda b, i: (b, jnp.maximum(i * per_tile - 1, 0), 0)),
            pl.BlockSpec((None, H, D_POOL), lambda b, i: (b, jnp.minimum((i + 1) * per_tile, n_halo - 1), 0)),
            y_spec, y_spec,
            mem_spec, mem_spec,
            _const_spec((1, D)),
            _const_spec((D, 2 * D)),
            _const_spec((len(POOL_WINDOWS), POOL_GROUP, POOL_GROUP)),
            _const_spec((1, D_POOL)),
            _const_spec((D_POOL, D)),
            _const_spec((D_SSM, 2 * D)),
            _const_spec((D, D)),
            _const_spec((1, D)),
            _const_spec((D, D)),
            _const_spec((D, D)),
        ],
        out_specs=tok,
        out_shape=jax.ShapeDtypeStruct((B, L, D), F32),
        scratch_shapes=[pltpu.VMEM((T + 2 * H, D_POOL), F32)],
        compiler_params=pltpu.CompilerParams(
            dimension_semantics=("arbitrary", "arbitrary"), vmem_limit_bytes=V7X_VMEM_LIMIT_BYTES),
        name="mix_attn",
    )(h, p, p, p, yf, yb, k, v, nmix, wgate, poolw, pscale, wpp, wglu, wmo, nx, wq, wxo)


def _block_diag_in(w):
    G, P, Hh = w.shape
    eye = jnp.eye(G, dtype=w.dtype)
    return jnp.einsum("gph,gk->ghkp", w, eye).reshape(G * Hh, G * P)


def _block_diag_out(w):
    G, Hh, P = w.shape
    eye = jnp.eye(G, dtype=w.dtype)
    return jnp.einsum("ghp,gk->gpkh", w, eye).reshape(G * P, G * Hh)


def kernel(x, mem, ffn1_norm, ffn1_w_gate, ffn1_w_up, ffn1_w_down, mix_norm, w_in, pool_w, pool_scale,
           w_pool_proj, ssm_a_re, ssm_a_im, ssm_log_dt, ssm_b_re, ssm_b_im, ssm_c_re, ssm_c_im, ssm_d,
           w_glu_val, w_glu_gate, w_mix_out, xattn_norm, mem_norm, w_q, w_kv, w_xo, ffn2_norm,
           ffn2_w_gate, ffn2_w_up, ffn2_w_down, final_norm):
    B, L, D = x.shape
    depth = ffn1_norm.shape[0]
    row = lambda v: v.reshape(1, -1)
    bf = lambda w: w.astype(BF16)
    h = x
    for l in range(depth):
        k_mem, v_mem = _kv_call(mem, row(mem_norm[l]), bf(w_kv[l]))

        h, p, s_tm = _ffn1_call(
            h, row(ffn1_norm[l]), bf(ffn1_w_gate[l]), bf(ffn1_w_up[l]), bf(ffn1_w_down[l]),
            row(mix_norm[l]), bf(w_in[l][:, :D_POOL + D_SSM]))

        G, P = N_SSM_GROUPS, SSM_STATE
        per_state = lambda a: a.reshape(2, 1, G * P)
        log_dt = jnp.broadcast_to(ssm_log_dt[l][:, :, None], (2, G, P))
        yf, yb = _ssm_call(
            s_tm.reshape(L * B, D_SSM),
            per_state(ssm_a_re[l]), per_state(ssm_a_im[l]), per_state(log_dt),
            jax.vmap(_block_diag_in)(ssm_b_re[l]), jax.vmap(_block_diag_in)(ssm_b_im[l]),
            bf(jax.vmap(_block_diag_out)(ssm_c_re[l])), bf(jax.vmap(_block_diag_out)(ssm_c_im[l])),
            row(ssm_d[l]))

        h = _mix_call(
            h, p, yf.reshape(L, B * D_SSM), yb.reshape(L, B * D_SSM), k_mem, v_mem,
            row(mix_norm[l]), bf(w_in[l][:, D_POOL + D_SSM:]), bf(pool_w[l]), row(pool_scale[l]),
            bf(w_pool_proj[l]), bf(jnp.concatenate([w_glu_val[l], w_glu_gate[l]], axis=1)),
            bf(w_mix_out[l]), row(xattn_norm[l]), bf(w_q[l]), bf(w_xo[l]))

        last = l == depth - 1
        assert last, "deeper stacks need a non-final variant of the second SwiGLU kernel"
        h = _ffn2_call(h, row(ffn2_norm[l]), bf(ffn2_w_gate[l]), bf(ffn2_w_up[l]), bf(ffn2_w_down[l]),
                       row(final_norm))
    return h
```

```python
import functools
import math

import jax
import jax.numpy as jnp
from jax import lax
from jax.experimental import pallas as pl
from jax.experimental.pallas import tpu as pltpu

F32 = jnp.float32
BF16 = jnp.bfloat16

D_MODEL = 1024
N_MEM = 256
D_FF = 2816
D_POOL = 512
POOL_WINDOWS = (2, 4, 8, 16)
POOL_GROUP = 128
D_SSM = 256
SSM_GROUP = 16
N_SSM_GROUPS = 16
SSM_STATE = 64
D_STATE = N_SSM_GROUPS * SSM_STATE
N_XHEADS = 4
XHEAD_DIM = 256
EPS = 1e-6

V7X_VMEM_LIMIT_BYTES = 56 * 1024 * 1024
V7X_SUBLANES = 8
POOL_HALO = 16

FFN_TILE = 512
MIX_TILE = 512
SSM_TILE = 64
FF_CHUNKS = ((0, 768), (768, 1536), (1536, 2304), (2304, 2816))
SSM_SUBTILE = 32
SCAN_LANES = 512


def _rms(v, g):
    r = lax.rsqrt(jnp.mean(v * v, axis=-1, keepdims=True) + EPS)
    return (v * r) * g


def _dot(a, b):
    return jnp.dot(a, b, preferred_element_type=F32)


def _const_spec(shape):
    nd = len(shape)
    return pl.BlockSpec(shape, lambda *_: (0,) * nd)


def _kv_kernel(mem_ref, g_ref, wkv_ref, k_ref, v_ref):
    m = _rms(mem_ref[...], g_ref[...]).astype(BF16)
    kv = _dot(m, wkv_ref[...])
    k_ref[...] = kv[:, :D_MODEL].astype(BF16)
    v_ref[...] = kv[:, D_MODEL:].astype(BF16)


def _kv_call(mem, mem_norm, w_kv):
    B = mem.shape[0]
    return pl.pallas_call(
        _kv_kernel,
        grid=(B,),
        in_specs=[
            pl.BlockSpec((None, N_MEM, D_MODEL), lambda b: (b, 0, 0)),
            _const_spec((1, D_MODEL)),
            _const_spec((D_MODEL, 2 * D_MODEL)),
        ],
        out_specs=[
            pl.BlockSpec((None, N_MEM, D_MODEL), lambda b: (b, 0, 0)),
            pl.BlockSpec((None, N_MEM, D_MODEL), lambda b: (b, 0, 0)),
        ],
        out_shape=[jax.ShapeDtypeStruct((B, N_MEM, D_MODEL), BF16)] * 2,
        compiler_params=pltpu.CompilerParams(
            dimension_semantics=("arbitrary",), vmem_limit_bytes=V7X_VMEM_LIMIT_BYTES),
        name="kv_proj",
    )(mem, mem_norm, w_kv)


def _swiglu_residual(x, g, wg_ref, wu_ref, wd_ref):
    u = _rms(x, g).astype(BF16)
    acc = None
    for c0, c1 in FF_CHUNKS:
        gate = _dot(u, wg_ref[:, c0:c1])
        up = _dot(u, wu_ref[:, c0:c1])
        act = (gate * jax.nn.sigmoid(gate) * up).astype(BF16)
        part = _dot(act, wd_ref[c0:c1, :])
        acc = part if acc is None else acc + part
    return x + 0.5 * acc


def _ffn1_kernel(x_ref, n1_ref, wg_ref, wu_ref, wd_ref, n2_ref, wps_ref, h_ref, p_ref, s_ref):
    h = _swiglu_residual(x_ref[...], n1_ref[...], wg_ref, wu_ref, wd_ref)
    h_ref[...] = h
    u = _rms(h, n2_ref[...]).astype(BF16)
    ps = _dot(u, wps_ref[...])
    p_ref[...] = ps[:, :D_POOL]
    s_ref[...] = ps[:, D_POOL:]


def _ffn1_call(x, n1, wg, wu, wd, n2, wps):
    B, L, D = x.shape
    T = FFN_TILE
    tok = pl.BlockSpec((None, T, D), lambda b, i: (b, i, 0))
    return pl.pallas_call(
        _ffn1_kernel,
        grid=(B, L // T),
        in_specs=[
            tok,
            _const_spec((1, D)),
            _const_spec((D, D_FF)),
            _const_spec((D, D_FF)),
            _const_spec((D_FF, D)),
            _const_spec((1, D)),
            _const_spec((D, D_POOL + D_SSM)),
        ],
        out_specs=[
            tok,
            pl.BlockSpec((None, T, D_POOL), lambda b, i: (b, i, 0)),
            pl.BlockSpec((None, T, D_SSM), lambda b, i: (b, i, 0)),
        ],
        out_shape=[
            jax.ShapeDtypeStruct((B, L, D), F32),
            jax.ShapeDtypeStruct((B, L, D_POOL), F32),
            jax.ShapeDtypeStruct((B, L, D_SSM), F32),
        ],
        compiler_params=pltpu.CompilerParams(
            dimension_semantics=("arbitrary", "arbitrary"), vmem_limit_bytes=V7X_VMEM_LIMIT_BYTES),
        name="ffn1",
    )(x, n1, wg, wu, wd, n2, wps)


def _ffn2_kernel(h_ref, n_ref, wg_ref, wu_ref, wd_ref, fn_ref, o_ref):
    h = _swiglu_residual(h_ref[...], n_ref[...], wg_ref, wu_ref, wd_ref)
    o_ref[...] = _rms(h, fn_ref[...])


def _ffn2_call(h, n, wg, wu, wd, fn):
    B, L, D = h.shape
    T = FFN_TILE
    tok = pl.BlockSpec((None, T, D), lambda b, i: (b, i, 0))
    return pl.pallas_call(
        _ffn2_kernel,
        grid=(B, L // T),
        in_specs=[
            tok,
            _const_spec((1, D)),
            _const_spec((D, D_FF)),
            _const_spec((D, D_FF)),
            _const_spec((D_FF, D)),
            _const_spec((1, D)),
        ],
        out_specs=tok,
        out_shape=jax.ShapeDtypeStruct((B, L, D), F32),
        compiler_params=pltpu.CompilerParams(
            dimension_semantics=("arbitrary", "arbitrary"), vmem_limit_bytes=V7X_VMEM_LIMIT_BYTES),
        name="ffn2",
    )(h, n, wg, wu, wd, fn)


def _ssm_kernel(sf_ref, sb_ref, are_ref, aim_ref, ldt_ref, bre_ref, bim_ref, crt_ref, cit_ref, d_ref,
                yf_ref, yb_ref, abar_scr, bmat_scr, cmat_scr, state_scr, bu_scr, xb_scr):
    i = pl.program_id(0)
    S = D_STATE
    n_b, steps, _ = sf_ref.shape
    R = steps * n_b

    @pl.when(i == 0)
    def _():
        for d in range(2):
            ar = are_ref[d]
            ai = aim_ref[d]
            dt = jnp.exp(ldt_ref[d])
            mag = jnp.exp(dt * ar)
            ang = dt * ai
            abr = mag * jnp.cos(ang)
            abi = mag * jnp.sin(ang)
            den = ar * ar + ai * ai
            nr = abr - 1.0
            qr = (nr * ar + abi * ai) / den
            qi = (abi * ar - nr * ai) / den
            abar_scr[d, 0] = jnp.broadcast_to(abr, (8, S))
            abar_scr[d, 1] = jnp.broadcast_to(abi, (8, S))
            br = bre_ref[d]
            bi = bim_ref[d]
            bmat_scr[d, :, :S] = (qr * br - qi * bi).astype(BF16)
            bmat_scr[d, :, S:] = (qr * bi + qi * br).astype(BF16)
            cmat_scr[d, :S, :] = crt_ref[d]
            cmat_scr[d, S:, :] = -cit_ref[d]
        state_scr[...] = jnp.zeros_like(state_scr)

    sub = SSM_SUBTILE
    n_sub = steps // sub
    halves = [(slice(c0, c0 + SCAN_LANES), slice(S + c0, S + c0 + SCAN_LANES))
              for c0 in range(0, S, SCAN_LANES)]
    s_refs = (sf_ref, sb_ref)
    order = (list(range(n_sub)), list(range(n_sub - 1, -1, -1)))

    for d in range(2):
        for j in order[d]:
            s_nat = s_refs[d][:, j * sub:(j + 1) * sub, :]
            s_tb = jnp.swapaxes(s_nat, 0, 1).reshape(sub * n_b, D_SSM)
            bu_scr[d, j * sub * n_b:(j + 1) * sub * n_b, :] = _dot(s_tb.astype(BF16), bmat_scr[d])

    for d, y_ref in ((0, yf_ref), (1, yb_ref)):
        coef = [(abar_scr[d, 0, :, re], abar_scr[d, 1, :, re]) for re, _ in halves]
        state = [(state_scr[d, 0, :, re], state_scr[d, 1, :, re]) for re, _ in halves]
        for j in order[d]:
            for hf, (re, im) in enumerate(halves):
                ar, ai = coef[hf]
                xr, xi = state[hf]
                for k in range(0, sub, 2):
                    lo = j * sub + (k if d == 0 else sub - 2 - k)
                    got = {}
                    for t in ((lo, lo + 1) if d == 0 else (lo + 1, lo)):
                        rows = slice(n_b * t, n_b * (t + 1))
                        nxr = ar * xr - ai * xi + bu_scr[d, rows, re]
                        nxi = ar * xi + ai * xr + bu_scr[d, rows, im]
                        xr, xi = nxr, nxi
                        got[t] = (nxr, nxi)
                    rows2 = slice(n_b * lo, n_b * (lo + 2))
                    xb_scr[d, rows2, re] = jnp.concatenate([got[lo][0], got[lo + 1][0]], axis=0).astype(BF16)
                    xb_scr[d, rows2, im] = jnp.concatenate([got[lo][1], got[lo + 1][1]], axis=0).astype(BF16)
                state[hf] = (xr, xi)
            rows_j = slice(j * sub * n_b, (j + 1) * sub * n_b)
            y_tb = _dot(xb_scr[d, rows_j, :], cmat_scr[d])
            y = jnp.swapaxes(y_tb.reshape(sub, n_b, D_SSM), 0, 1)
            if d == 0:
                y = y + sf_ref[:, j * sub:(j + 1) * sub, :] * d_ref[...]
            y_ref[:, j * sub:(j + 1) * sub, :] = y
        for hf, (re, _) in enumerate(halves):
            state_scr[d, 0, :, re] = state[hf][0]
            state_scr[d, 1, :, re] = state[hf][1]


def _ssm_call(s, a_re, a_im, log_dt, b_re_bd, b_im_bd, c_re_t, c_im_t, d_skip):
    B, L, _ = s.shape
    assert B == V7X_SUBLANES, "the scan keeps one batch element per sublane"
    T = SSM_TILE
    n = L // T
    S = D_STATE
    fwd = pl.BlockSpec((B, T, D_SSM), lambda i: (0, i, 0))
    bwd = pl.BlockSpec((B, T, D_SSM), lambda i: (0, n - 1 - i, 0))
    return pl.pallas_call(
        _ssm_kernel,
        grid=(n,),
        in_specs=[
            fwd, bwd,
            _const_spec((2, 1, S)), _const_spec((2, 1, S)), _const_spec((2, 1, S)),
            _const_spec((2, D_SSM, S)), _const_spec((2, D_SSM, S)),
            _const_spec((2, S, D_SSM)), _const_spec((2, S, D_SSM)),
            _const_spec((1, D_SSM)),
        ],
        out_specs=[fwd, bwd],
        out_shape=[jax.ShapeDtypeStruct((B, L, D_SSM), F32)] * 2,
        scratch_shapes=[
            pltpu.VMEM((2, 2, 8, S), F32),
            pltpu.VMEM((2, D_SSM, 2 * S), BF16),
            pltpu.VMEM((2, 2 * S, D_SSM), BF16),
            pltpu.VMEM((2, 2, 8, S), F32),
            pltpu.VMEM((2, B * T, 2 * S), F32),
            pltpu.VMEM((2, B * T, 2 * S), BF16),
        ],
        compiler_params=pltpu.CompilerParams(
            dimension_semantics=("arbitrary",), vmem_limit_bytes=V7X_VMEM_LIMIT_BYTES),
        name="ssm_scan",
    )(s, s, a_re, a_im, log_dt, b_re_bd, b_im_bd, c_re_t, c_im_t, d_skip)


def _mix_kernel(h_ref, p_ref, pl_ref, pr_ref, yf_ref, yb_ref, k_ref, v_ref,
                nmix_ref, wgate_ref, poolw_ref, pscale_ref, wpp_ref, wglu_ref, wmo_ref,
                nx_ref, wq_ref, wxo_ref, o_ref, pext_scr, *, seq_len):
    i = pl.program_id(1)
    n_i = pl.num_programs(1)
    T = h_ref.shape[0]
    H = POOL_HALO
    h = h_ref[...]

    pext_scr[0:H, :] = jnp.where(i > 0, pl_ref[...], 0.0)
    pext_scr[H:H + T, :] = p_ref[...]
    pext_scr[H + T:, :] = jnp.where(i < n_i - 1, pr_ref[...], 0.0)
    t_glob = i * T + lax.broadcasted_iota(jnp.int32, (T, POOL_GROUP), 0)
    mixed = []
    for g, w in enumerate(POOL_WINDOWS):
        left = w // 2
        right = w - 1 - left
        lanes = slice(g * POOL_GROUP, (g + 1) * POOL_GROUP)
        tot = None
        for dlt in range(-left, right + 1):
            piece = pext_scr[H + dlt:H + dlt + T, lanes]
            tot = piece if tot is None else tot + piece
        hi = jnp.minimum(t_glob + (right + 1), seq_len)
        lo = jnp.maximum(t_glob - left, 0)
        cnt = (hi - lo).astype(F32)
        pooled = tot / cnt - pext_scr[H:H + T, lanes]
        mixed.append(_dot(pooled.astype(BF16), poolw_ref[g]))
    mixed = jnp.concatenate(mixed, axis=-1) * pscale_ref[...]
    z_pool = _dot(mixed.astype(BF16), wpp_ref[...])

    y = jax.nn.gelu(yf_ref[...] + yb_ref[...]).astype(BF16)
    glu = _dot(y, wglu_ref[...])
    z_ssm = glu[:, :D_MODEL] * jax.nn.sigmoid(glu[:, D_MODEL:])

    u = _rms(h, nmix_ref[...]).astype(BF16)
    gates = _dot(u, wgate_ref[...])
    merged = (jax.nn.sigmoid(gates[:, :D_MODEL]) * z_pool
              + jax.nn.sigmoid(gates[:, D_MODEL:]) * z_ssm)
    h = h + _dot(merged.astype(BF16), wmo_ref[...])

    q = _dot(_rms(h, nx_ref[...]).astype(BF16), wq_ref[...])
    heads = []
    for hd in range(N_XHEADS):
        cols = slice(hd * XHEAD_DIM, (hd + 1) * XHEAD_DIM)
        qh = q[:, cols].astype(BF16)
        sc = lax.dot_general(qh, k_ref[:, cols], (((1,), (1,)), ((), ())),
                             preferred_element_type=F32) / math.sqrt(XHEAD_DIM)
        e = jnp.exp(sc - jnp.max(sc, axis=-1, keepdims=True))
        probs = e / jnp.sum(e, axis=-1, keepdims=True)
        heads.append(_dot(probs.astype(BF16), v_ref[:, cols]))
    o = jnp.concatenate(heads, axis=-1).astype(BF16)
    o_ref[...] = h + _dot(o, wxo_ref[...])


def _mix_call(h, p, yf, yb, k, v, nmix, wgate, poolw, pscale, wpp, wglu, wmo, nx, wq, wxo):
    B, L, D = h.shape
    T = MIX_TILE
    H = POOL_HALO
    per_tile = T // H
    n_halo = L // H
    tok = pl.BlockSpec((None, T, D), lambda b, i: (b, i, 0))
    y_spec = pl.BlockSpec((None, T, D_SSM), lambda b, i: (b, i, 0))
    mem_spec = pl.BlockSpec((None, N_MEM, D), lambda b, i: (b, 0, 0))
    return pl.pallas_call(
        functools.partial(_mix_kernel, seq_len=L),
        grid=(B, L // T),
        in_specs=[
            tok,
            pl.BlockSpec((None, T, D_POOL), lambda b, i: (b, i, 0)),
            pl.BlockSpec((None, H, D_POOL), lambda b, i: (b, jnp.maximum(i * per_tile - 1, 0), 0)),
            pl.BlockSpec((None, H, D_POOL), lambda b, i: (b, jnp.minimum((i + 1) * per_tile, n_halo - 1), 0)),
            y_spec, y_spec,
            mem_spec, mem_spec,
            _const_spec((1, D)),
            _const_spec((D, 2 * D)),
            _const_spec((len(POOL_WINDOWS), POOL_GROUP, POOL_GROUP)),
            _const_spec((1, D_POOL)),
            _const_spec((D_POOL, D)),
            _const_spec((D_SSM, 2 * D)),
            _const_spec((D, D)),
            _const_spec((1, D)),
            _const_spec((D, D)),
            _const_spec((D, D)),
        ],
        out_specs=tok,
        out_shape=jax.ShapeDtypeStruct((B, L, D), F32),
        scratch_shapes=[pltpu.VMEM((T + 2 * H, D_POOL), F32)],
        compiler_params=pltpu.CompilerParams(
            dimension_semantics=("arbitrary", "arbitrary"), vmem_limit_bytes=V7X_VMEM_LIMIT_BYTES),
        name="mix_attn",
    )(h, p, p, p, yf, yb, k, v, nmix, wgate, poolw, pscale, wpp, wglu, wmo, nx, wq, wxo)


def _block_diag_in(w):
    G, P, Hh = w.shape
    eye = jnp.eye(G, dtype=w.dtype)
    return jnp.einsum("gph,gk->ghkp", w, eye).reshape(G * Hh, G * P)


def _block_diag_out(w):
    G, Hh, P = w.shape
    eye = jnp.eye(G, dtype=w.dtype)
    return jnp.einsum("ghp,gk->gpkh", w, eye).reshape(G * P, G * Hh)


def kernel(x, mem, ffn1_norm, ffn1_w_gate, ffn1_w_up, ffn1_w_down, mix_norm, w_in, pool_w, pool_scale,
           w_pool_proj, ssm_a_re, ssm_a_im, ssm_log_dt, ssm_b_re, ssm_b_im, ssm_c_re, ssm_c_im, ssm_d,
           w_glu_val, w_glu_gate, w_mix_out, xattn_norm, mem_norm, w_q, w_kv, w_xo, ffn2_norm,
           ffn2_w_gate, ffn2_w_up, ffn2_w_down, final_norm):
    B, L, D = x.shape
    depth = ffn1_norm.shape[0]
    row = lambda v: v.reshape(1, -1)
    bf = lambda w: w.astype(BF16)
    h = x
    for l in range(depth):
        k_mem, v_mem = _kv_call(mem, row(mem_norm[l]), bf(w_kv[l]))

        h, p, s = _ffn1_call(
            h, row(ffn1_norm[l]), bf(ffn1_w_gate[l]), bf(ffn1_w_up[l]), bf(ffn1_w_down[l]),
            row(mix_norm[l]), bf(w_in[l][:, :D_POOL + D_SSM]))

        G, P = N_SSM_GROUPS, SSM_STATE
        per_state = lambda a: a.reshape(2, 1, G * P)
        log_dt = jnp.broadcast_to(ssm_log_dt[l][:, :, None], (2, G, P))
        yf, yb = _ssm_call(
            s, per_state(ssm_a_re[l]), per_state(ssm_a_im[l]), per_state(log_dt),
            jax.vmap(_block_diag_in)(ssm_b_re[l]), jax.vmap(_block_diag_in)(ssm_b_im[l]),
            bf(jax.vmap(_block_diag_out)(ssm_c_re[l])), bf(jax.vmap(_block_diag_out)(ssm_c_im[l])),
            row(ssm_d[l]))

        h = _mix_call(
            h, p, yf, yb, k_mem, v_mem,
            row(mix_norm[l]), bf(w_in[l][:, D_POOL + D_SSM:]), bf(pool_w[l]), row(pool_scale[l]),
            bf(w_pool_proj[l]), bf(jnp.concatenate([w_glu_val[l], w_glu_gate[l]], axis=1)),
            bf(w_mix_out[l]), row(xattn_norm[l]), bf(w_q[l]), bf(w_xo[l]))

        last = l == depth - 1
        assert last, "deeper stacks need a non-final variant of the second SwiGLU kernel"
        h = _ffn2_call(h, row(ffn2_norm[l]), bf(ffn2_w_gate[l]), bf(ffn2_w_up[l]), bf(ffn2_w_down[l]),
                       row(final_norm))
    return h
```

```python
import functools
import math

import jax
import jax.numpy as jnp
from jax import lax
from jax.experimental import pallas as pl
from jax.experimental.pallas import tpu as pltpu

F32 = jnp.float32
BF16 = jnp.bfloat16

D_MODEL = 1024
N_MEM = 256
D_FF = 2816
D_POOL = 512
POOL_WINDOWS = (2, 4, 8, 16)
POOL_GROUP = 128
D_SSM = 256
SSM_GROUP = 16
N_SSM_GROUPS = 16
SSM_STATE = 64
D_STATE = N_SSM_GROUPS * SSM_STATE
N_XHEADS = 4
XHEAD_DIM = 256
EPS = 1e-6

V7X_VMEM_LIMIT_BYTES = 56 * 1024 * 1024
V7X_SUBLANES = 8
POOL_HALO = 16

FFN_TILE = 512
MIX_TILE = 512
SSM_TILE = 64
FF_CHUNKS = ((0, 768), (768, 1536), (1536, 2304), (2304, 2816))
SSM_SUBTILE = 32
SCAN_LANES = 512


def _rms(v, g):
    r = lax.rsqrt(jnp.mean(v * v, axis=-1, keepdims=True) + EPS)
    return (v * r) * g


def _dot(a, b):
    return jnp.dot(a, b, preferred_element_type=F32)


def _const_spec(shape):
    nd = len(shape)
    return pl.BlockSpec(shape, lambda *_: (0,) * nd)


def _kv_kernel(mem_ref, g_ref, wkv_ref, k_ref, v_ref):
    m = _rms(mem_ref[...], g_ref[...]).astype(BF16)
    kv = _dot(m, wkv_ref[...])
    k_ref[...] = kv[:, :D_MODEL].astype(BF16)
    v_ref[...] = kv[:, D_MODEL:].astype(BF16)


def _kv_call(mem, mem_norm, w_kv):
    B = mem.shape[0]
    return pl.pallas_call(
        _kv_kernel,
        grid=(B,),
        in_specs=[
            pl.BlockSpec((None, N_MEM, D_MODEL), lambda b: (b, 0, 0)),
            _const_spec((1, D_MODEL)),
            _const_spec((D_MODEL, 2 * D_MODEL)),
        ],
        out_specs=[
            pl.BlockSpec((None, N_MEM, D_MODEL), lambda b: (b, 0, 0)),
            pl.BlockSpec((None, N_MEM, D_MODEL), lambda b: (b, 0, 0)),
        ],
        out_shape=[jax.ShapeDtypeStruct((B, N_MEM, D_MODEL), BF16)] * 2,
        compiler_params=pltpu.CompilerParams(
            dimension_semantics=("arbitrary",), vmem_limit_bytes=V7X_VMEM_LIMIT_BYTES),
        name="kv_proj",
    )(mem, mem_norm, w_kv)


def _swiglu_residual(x, g, wg_ref, wu_ref, wd_ref):
    u = _rms(x, g).astype(BF16)
    acc = None
    for c0, c1 in FF_CHUNKS:
        gate = _dot(u, wg_ref[:, c0:c1])
        up = _dot(u, wu_ref[:, c0:c1])
        act = (gate * jax.nn.sigmoid(gate) * up).astype(BF16)
        part = _dot(act, wd_ref[c0:c1, :])
        acc = part if acc is None else acc + part
    return x + 0.5 * acc


def _ffn1_kernel(x_ref, n1_ref, wg_ref, wu_ref, wd_ref, n2_ref, wps_ref, h_ref, p_ref, s_ref):
    h = _swiglu_residual(x_ref[...], n1_ref[...], wg_ref, wu_ref, wd_ref)
    h_ref[...] = h
    u = _rms(h, n2_ref[...]).astype(BF16)
    ps = _dot(u, wps_ref[...])
    p_ref[...] = ps[:, :D_POOL]
    s_ref[...] = ps[:, D_POOL:]


def _ffn1_call(x, n1, wg, wu, wd, n2, wps):
    B, L, D = x.shape
    T = FFN_TILE
    tok = pl.BlockSpec((None, T, D), lambda b, i: (b, i, 0))
    return pl.pallas_call(
        _ffn1_kernel,
        grid=(B, L // T),
        in_specs=[
            tok,
            _const_spec((1, D)),
            _const_spec((D, D_FF)),
            _const_spec((D, D_FF)),
            _const_spec((D_FF, D)),
            _const_spec((1, D)),
            _const_spec((D, D_POOL + D_SSM)),
        ],
        out_specs=[
            tok,
            pl.BlockSpec((None, T, D_POOL), lambda b, i: (b, i, 0)),
            pl.BlockSpec((None, T, D_SSM), lambda b, i: (b, i, 0)),
        ],
        out_shape=[
            jax.ShapeDtypeStruct((B, L, D), F32),
            jax.ShapeDtypeStruct((B, L, D_POOL), F32),
            jax.ShapeDtypeStruct((B, L, D_SSM), F32),
        ],
        compiler_params=pltpu.CompilerParams(
            dimension_semantics=("arbitrary", "arbitrary"), vmem_limit_bytes=V7X_VMEM_LIMIT_BYTES),
        name="ffn1",
    )(x, n1, wg, wu, wd, n2, wps)


def _ffn2_kernel(h_ref, n_ref, wg_ref, wu_ref, wd_ref, fn_ref, o_ref):
    h = _swiglu_residual(h_ref[...], n_ref[...], wg_ref, wu_ref, wd_ref)
    o_ref[...] = _rms(h, fn_ref[...])


def _ffn2_call(h, n, wg, wu, wd, fn):
    B, L, D = h.shape
    T = FFN_TILE
    tok = pl.BlockSpec((None, T, D), lambda b, i: (b, i, 0))
    return pl.pallas_call(
        _ffn2_kernel,
        grid=(B, L // T),
        in_specs=[
            tok,
            _const_spec((1, D)),
            _const_spec((D, D_FF)),
            _const_spec((D, D_FF)),
            _const_spec((D_FF, D)),
            _const_spec((1, D)),
        ],
        out_specs=tok,
        out_shape=jax.ShapeDtypeStruct((B, L, D), F32),
        compiler_params=pltpu.CompilerParams(
            dimension_semantics=("arbitrary", "arbitrary"), vmem_limit_bytes=V7X_VMEM_LIMIT_BYTES),
        name="ffn2",
    )(h, n, wg, wu, wd, fn)


def _ssm_kernel(sf_ref, sb_ref, are_ref, aim_ref, ldt_ref, bre_ref, bim_ref, crt_ref, cit_ref, d_ref,
                yf_ref, yb_ref, abar_scr, bmat_scr, cmat_scr, state_scr, bu_scr, xb_scr):
    i = pl.program_id(0)
    S = D_STATE
    n_b, steps, _ = sf_ref.shape
    R = steps * n_b

    @pl.when(i == 0)
    def _():
        for d in range(2):
            ar = are_ref[d]
            ai = aim_ref[d]
            dt = jnp.exp(ldt_ref[d])
            mag = jnp.exp(dt * ar)
            ang = dt * ai
            abr = mag * jnp.cos(ang)
            abi = mag * jnp.sin(ang)
            den = ar * ar + ai * ai
            nr = abr - 1.0
            qr = (nr * ar + abi * ai) / den
            qi = (abi * ar - nr * ai) / den
            abar_scr[d, 0] = jnp.broadcast_to(abr, (8, S))
            abar_scr[d, 1] = jnp.broadcast_to(abi, (8, S))
            br = bre_ref[d]
            bi = bim_ref[d]
            bmat_scr[d, :, :S] = (qr * br - qi * bi).astype(BF16)
            bmat_scr[d, :, S:] = (qr * bi + qi * br).astype(BF16)
            cmat_scr[d, :S, :] = crt_ref[d]
            cmat_scr[d, S:, :] = -cit_ref[d]
        state_scr[...] = jnp.zeros_like(state_scr)

    sub = SSM_SUBTILE
    n_sub = steps // sub
    halves = [(slice(c0, c0 + SCAN_LANES), slice(S + c0, S + c0 + SCAN_LANES))
              for c0 in range(0, S, SCAN_LANES)]
    s_refs = (sf_ref, sb_ref)
    order = (list(range(n_sub)), list(range(n_sub - 1, -1, -1)))

    for d in range(2):
        for j in order[d]:
            s_nat = s_refs[d][:, j * sub:(j + 1) * sub, :]
            s_tb = jnp.swapaxes(s_nat, 0, 1).reshape(sub * n_b, D_SSM)
            bu_scr[d, j * sub * n_b:(j + 1) * sub * n_b, :] = _dot(s_tb.astype(BF16), bmat_scr[d])

    for d, y_ref in ((0, yf_ref), (1, yb_ref)):
        coef = [(abar_scr[d, 0, :, re], abar_scr[d, 1, :, re]) for re, _ in halves]
        state = [(state_scr[d, 0, :, re], state_scr[d, 1, :, re]) for re, _ in halves]
        for j in order[d]:
            for hf, (re, im) in enumerate(halves):
                ar, ai = coef[hf]
                xr, xi = state[hf]
                for k in range(0, sub, 2):
                    lo = j * sub + (k if d == 0 else sub - 2 - k)
                    got = {}
                    for t in ((lo, lo + 1) if d == 0 else (lo + 1, lo)):
                        rows = slice(n_b * t, n_b * (t + 1))
                        nxr = ar * xr - ai * xi + bu_scr[d, rows, re]
                        nxi = ar * xi + ai * xr + bu_scr[d, rows, im]
                        xr, xi = nxr, nxi
                        got[t] = (nxr, nxi)
                    rows2 = slice(n_b * lo, n_b * (lo + 2))
                    xb_scr[d, rows2, re] = jnp.concatenate([got[lo][0], got[lo + 1][0]], axis=0).astype(BF16)
                    xb_scr[d, rows2, im] = jnp.concatenate([got[lo][1], got[lo + 1][1]], axis=0).astype(BF16)
                state[hf] = (xr, xi)
            rows_j = slice(j * sub * n_b, (j + 1) * sub * n_b)
            y_tb = _dot(xb_scr[d, rows_j, :], cmat_scr[d])
            y = jnp.swapaxes(y_tb.reshape(sub, n_b, D_SSM), 0, 1)
            if d == 0:
                y = y + sf_ref[:, j * sub:(j + 1) * sub, :] * d_ref[...]
            y_ref[:, j * sub:(j + 1) * sub, :] = y
        for hf, (re, _) in enumerate(halves):
            state_scr[d, 0, :, re] = state[hf][0]
            state_scr[d, 1, :, re] = state[hf][1]


def _ssm_call(s, a_re, a_im, log_dt, b_re_bd, b_im_bd, c_re_t, c_im_t, d_skip):
    B, L, _ = s.shape
    assert B == V7X_SUBLANES, "the scan keeps one batch element per sublane"
    T = SSM_TILE
    n = L // T
    S = D_STATE
    fwd = pl.BlockSpec((B, T, D_SSM), lambda i: (0, i, 0))
    bwd = pl.BlockSpec((B, T, D_SSM), lambda i: (0, n - 1 - i, 0))
    return pl.pallas_call(
        _ssm_kernel,
        grid=(n,),
        in_specs=[
            fwd, bwd,
            _const_spec((2, 1, S)), _const_spec((2, 1, S)), _const_spec((2, 1, S)),
            _const_spec((2, D_SSM, S)), _const_spec((2, D_SSM, S)),
            _const_spec((2, S, D_SSM)), _const_spec((2, S, D_SSM)),
            _const_spec((1, D_SSM)),
        ],
        out_specs=[fwd, bwd],
        out_shape=[jax.ShapeDtypeStruct((B, L, D_SSM), F32)] * 2,
        scratch_shapes=[
            pltpu.VMEM((2, 2, 8, S), F32),
            pltpu.VMEM((2, D_SSM, 2 * S), BF16),
            pltpu.VMEM((2, 2 * S, D_SSM), BF16),
            pltpu.VMEM((2, 2, 8, S), F32),
            pltpu.VMEM((2, B * T, 2 * S), F32),
            pltpu.VMEM((2, B * T, 2 * S), BF16),
        ],
        compiler_params=pltpu.CompilerParams(
            dimension_semantics=("arbitrary",), vmem_limit_bytes=V7X_VMEM_LIMIT_BYTES),
        name="ssm_scan",
    )(s, s, a_re, a_im, log_dt, b_re_bd, b_im_bd, c_re_t, c_im_t, d_skip)


def _window_sum(v, w):
    n = v.shape[0]
    ahead = lambda a, k: pltpu.roll(a, n - k, axis=0)
    behind = lambda a, k: pltpu.roll(a, k, axis=0)
    if w == 2:
        return v + behind(v, 1)
    cur, span = v + ahead(v, 1), 2
    while span < w // 2:
        cur, span = cur + ahead(cur, span), 2 * span
    return cur + behind(cur, span)


def _mix_kernel(h_ref, p_ref, pl_ref, pr_ref, yf_ref, yb_ref, k_ref, v_ref,
                nmix_ref, wgate_ref, poolw_ref, pscale_ref, wpp_ref, wglu_ref, wmo_ref,
                nx_ref, wq_ref, wxo_ref, o_ref, *, seq_len):
    i = pl.program_id(1)
    n_i = pl.num_programs(1)
    T = h_ref.shape[0]
    H = POOL_HALO
    h = h_ref[...]

    halo_l = jnp.where(i > 0, pl_ref[...], 0.0)
    halo_r = jnp.where(i < n_i - 1, pr_ref[...], 0.0)
    edge = V7X_SUBLANES
    t_edge = lax.broadcasted_iota(jnp.int32, (edge, POOL_GROUP), 0)
    mixed = []
    for g, w in enumerate(POOL_WINDOWS):
        left = w // 2
        right = w - 1 - left
        lanes = slice(g * POOL_GROUP, (g + 1) * POOL_GROUP)
        v = jnp.concatenate([halo_l[:, lanes], p_ref[:, lanes], halo_r[:, lanes]], axis=0)
        tot = _window_sum(v, w)[H:H + T]
        self_ = v[H:H + T]

        def clipped(t0, rows):
            t = i * T + t0 + t_edge
            cnt = jnp.minimum(t + (right + 1), seq_len) - jnp.maximum(t - left, 0)
            return tot[rows] / cnt.astype(F32) - self_[rows]

        pooled = jnp.concatenate([
            clipped(0, slice(0, edge)),
            tot[edge:T - edge] * (1.0 / w) - self_[edge:T - edge],
            clipped(T - edge, slice(T - edge, T)),
        ], axis=0)
        mixed.append(_dot(pooled.astype(BF16), poolw_ref[g]))
    mixed = jnp.concatenate(mixed, axis=-1) * pscale_ref[...]
    z_pool = _dot(mixed.astype(BF16), wpp_ref[...])

    y = jax.nn.gelu(yf_ref[...] + yb_ref[...]).astype(BF16)
    glu = _dot(y, wglu_ref[...])
    z_ssm = glu[:, :D_MODEL] * jax.nn.sigmoid(glu[:, D_MODEL:])

    u = _rms(h, nmix_ref[...]).astype(BF16)
    gates = _dot(u, wgate_ref[...])
    merged = (jax.nn.sigmoid(gates[:, :D_MODEL]) * z_pool
              + jax.nn.sigmoid(gates[:, D_MODEL:]) * z_ssm)
    h = h + _dot(merged.astype(BF16), wmo_ref[...])

    q = _dot(_rms(h, nx_ref[...]).astype(BF16), wq_ref[...])
    heads = []
    for hd in range(N_XHEADS):
        cols = slice(hd * XHEAD_DIM, (hd + 1) * XHEAD_DIM)
        qh = q[:, cols].astype(BF16)
        sc = lax.dot_general(qh, k_ref[:, cols], (((1,), (1,)), ((), ())),
                             preferred_element_type=F32) / math.sqrt(XHEAD_DIM)
        e = jnp.exp(sc - jnp.max(sc, axis=-1, keepdims=True))
        probs = e / jnp.sum(e, axis=-1, keepdims=True)
        heads.append(_dot(probs.astype(BF16), v_ref[:, cols]))
    o = jnp.concatenate(heads, axis=-1).astype(BF16)
    o_ref[...] = h + _dot(o, wxo_ref[...])


def _mix_call(h, p, yf, yb, k, v, nmix, wgate, poolw, pscale, wpp, wglu, wmo, nx, wq, wxo):
    B, L, D = h.shape
    T = MIX_TILE
    H = POOL_HALO
    per_tile = T // H
    n_halo = L // H
    tok = pl.BlockSpec((None, T, D), lambda b, i: (b, i, 0))
    y_spec = pl.BlockSpec((None, T, D_SSM), lambda b, i: (b, i, 0))
    mem_spec = pl.BlockSpec((None, N_MEM, D), lambda b, i: (b, 0, 0))
    return pl.pallas_call(
        functools.partial(_mix_kernel, seq_len=L),
        grid=(B, L // T),
        in_specs=[
            tok,
            pl.BlockSpec((None, T, D_POOL), lambda b, i: (b, i, 0)),
            pl.BlockSpec((None, H, D_POOL), lambda b, i: (b, jnp.maximum(i * per_tile - 1, 0), 0)),
            pl.BlockSpec((None, H, D_POOL), lambda b, i: (b, jnp.minimum((i + 1) * per_tile, n_halo - 1), 0)),
            y_spec, y_spec,
            mem_spec, mem_spec,
            _const_spec((1, D)),
            _const_spec((D, 2 * D)),
            _const_spec((len(POOL_WINDOWS), POOL_GROUP, POOL_GROUP)),
            _const_spec((1, D_POOL)),
            _const_spec((D_POOL, D)),
            _const_spec((D_SSM, 2 * D)),
            _const_spec((D, D)),
            _const_spec((1, D)),
            _const_spec((D, D)),
            _const_spec((D, D)),
        ],
        out_specs=tok,
        out_shape=jax.ShapeDtypeStruct((B, L, D), F32),
        compiler_params=pltpu.CompilerParams(
            dimension_semantics=("arbitrary", "arbitrary"), vmem_limit_bytes=V7X_VMEM_LIMIT_BYTES),
        name="mix_attn",
    )(h, p, p, p, yf, yb, k, v, nmix, wgate, poolw, pscale, wpp, wglu, wmo, nx, wq, wxo)


def _block_diag_in(w):
    G, P, Hh = w.shape
    eye = jnp.eye(G, dtype=w.dtype)
    return jnp.einsum("gph,gk->ghkp", w, eye).reshape(G * Hh, G * P)


def _block_diag_out(w):
    G, Hh, P = w.shape
    eye = jnp.eye(G, dtype=w.dtype)
    return jnp.einsum("ghp,gk->gpkh", w, eye).reshape(G * P, G * Hh)


def kernel(x, mem, ffn1_norm, ffn1_w_gate, ffn1_w_up, ffn1_w_down, mix_norm, w_in, pool_w, pool_scale,
           w_pool_proj, ssm_a_re, ssm_a_im, ssm_log_dt, ssm_b_re, ssm_b_im, ssm_c_re, ssm_c_im, ssm_d,
           w_glu_val, w_glu_gate, w_mix_out, xattn_norm, mem_norm, w_q, w_kv, w_xo, ffn2_norm,
           ffn2_w_gate, ffn2_w_up, ffn2_w_down, final_norm):
    B, L, D = x.shape
    depth = ffn1_norm.shape[0]
    row = lambda v: v.reshape(1, -1)
    bf = lambda w: w.astype(BF16)
    h = x
    for l in range(depth):
        k_mem, v_mem = _kv_call(mem, row(mem_norm[l]), bf(w_kv[l]))

        h, p, s = _ffn1_call(
            h, row(ffn1_norm[l]), bf(ffn1_w_gate[l]), bf(ffn1_w_up[l]), bf(ffn1_w_down[l]),
            row(mix_norm[l]), bf(w_in[l][:, :D_POOL + D_SSM]))

        G, P = N_SSM_GROUPS, SSM_STATE
        per_state = lambda a: a.reshape(2, 1, G * P)
        log_dt = jnp.broadcast_to(ssm_log_dt[l][:, :, None], (2, G, P))
        yf, yb = _ssm_call(
            s, per_state(ssm_a_re[l]), per_state(ssm_a_im[l]), per_state(log_dt),
            jax.vmap(_block_diag_in)(ssm_b_re[l]), jax.vmap(_block_diag_in)(ssm_b_im[l]),
            bf(jax.vmap(_block_diag_out)(ssm_c_re[l])), bf(jax.vmap(_block_diag_out)(ssm_c_im[l])),
            row(ssm_d[l]))

        h = _mix_call(
            h, p, yf, yb, k_mem, v_mem,
            row(mix_norm[l]), bf(w_in[l][:, D_POOL + D_SSM:]), bf(pool_w[l]), row(pool_scale[l]),
            bf(w_pool_proj[l]), bf(jnp.concatenate([w_glu_val[l], w_glu_gate[l]], axis=1)),
            bf(w_mix_out[l]), row(xattn_norm[l]), bf(w_q[l]), bf(w_xo[l]))

        last = l == depth - 1
        assert last, "deeper stacks need a non-final variant of the second SwiGLU kernel"
        h = _ffn2_call(h, row(ffn2_norm[l]), bf(ffn2_w_gate[l]), bf(ffn2_w_up[l]), bf(ffn2_w_down[l]),
                       row(final_norm))
    return h
```

```python
import functools
import math

import jax
import jax.numpy as jnp
from jax import lax
from jax.experimental import pallas as pl
from jax.experimental.pallas import tpu as pltpu

F32 = jnp.float32
BF16 = jnp.bfloat16

D_MODEL = 1024
N_MEM = 256
D_FF = 2816
D_POOL = 512
POOL_WINDOWS = (2, 4, 8, 16)
POOL_GROUP = 128
D_SSM = 256
SSM_GROUP = 16
N_SSM_GROUPS = 16
SSM_STATE = 64
D_STATE = N_SSM_GROUPS * SSM_STATE
N_XHEADS = 4
XHEAD_DIM = 256
EPS = 1e-6

V7X_VMEM_LIMIT_BYTES = 56 * 1024 * 1024
V7X_SUBLANES = 8
POOL_HALO = 16

FFN_TILE = 1024
MIX_TILE = 1024
SSM_TILE = 64
FFN_SKEW = 1
MIX_SKEW = 1
GROUP_ROWS = 256
FF_CHUNKS = ((0, 768), (768, 1536), (1536, 2304), (2304, 2816))
SSM_SUBTILE = 32
SCAN_LANES = 512


def _rms(v, g):
    r = lax.rsqrt(jnp.mean(v * v, axis=-1, keepdims=True) + EPS)
    return (v * r) * g


def _dot(a, b):
    return jnp.dot(a, b, preferred_element_type=F32)


def _const_spec(shape):
    nd = len(shape)
    return pl.BlockSpec(shape, lambda *_: (0,) * nd)


def _kv_kernel(mem_ref, g_ref, wkv_ref, k_ref, v_ref):
    m = _rms(mem_ref[...], g_ref[...]).astype(BF16)
    kv = _dot(m, wkv_ref[...])
    k_ref[...] = kv[:, :D_MODEL].astype(BF16)
    v_ref[...] = kv[:, D_MODEL:].astype(BF16)


def _kv_call(mem, mem_norm, w_kv):
    B = mem.shape[0]
    return pl.pallas_call(
        _kv_kernel,
        grid=(B,),
        in_specs=[
            pl.BlockSpec((None, N_MEM, D_MODEL), lambda b: (b, 0, 0)),
            _const_spec((1, D_MODEL)),
            _const_spec((D_MODEL, 2 * D_MODEL)),
        ],
        out_specs=[
            pl.BlockSpec((None, N_MEM, D_MODEL), lambda b: (b, 0, 0)),
            pl.BlockSpec((None, N_MEM, D_MODEL), lambda b: (b, 0, 0)),
        ],
        out_shape=[jax.ShapeDtypeStruct((B, N_MEM, D_MODEL), BF16)] * 2,
        compiler_params=pltpu.CompilerParams(
            dimension_semantics=("arbitrary",), vmem_limit_bytes=V7X_VMEM_LIMIT_BYTES),
        name="kv_proj",
    )(mem, mem_norm, w_kv)


def _row_groups(n_rows):
    return [slice(r, r + GROUP_ROWS) for r in range(0, n_rows, GROUP_ROWS)]


def _trace_round_robin(chains, skew=0):
    waiting = list(chains)
    n_started = 0
    live = []
    turn = 0
    while waiting or live:
        while waiting and turn >= n_started * skew:
            live.append(waiting.pop(0))
            n_started += 1
        for c in list(live):
            try:
                next(c)
            except StopIteration:
                live.remove(c)
        turn += 1


def _swiglu_residual(x, g, wg_ref, wu_ref, wd_ref):
    u = _rms(x, g).astype(BF16)
    acc = None
    for c0, c1 in FF_CHUNKS:
        gate = _dot(u, wg_ref[:, c0:c1])
        up = _dot(u, wu_ref[:, c0:c1])
        yield
        act = (gate * jax.nn.sigmoid(gate) * up).astype(BF16)
        part = _dot(act, wd_ref[c0:c1, :])
        acc = part if acc is None else acc + part
        yield
    return x + 0.5 * acc


def _ffn1_kernel(x_ref, n1_ref, wg_ref, wu_ref, wd_ref, n2_ref, wps_ref, h_ref, p_ref, s_ref):
    def chain(rows):
        h = yield from _swiglu_residual(x_ref[rows, :], n1_ref[...], wg_ref, wu_ref, wd_ref)
        h_ref[rows, :] = h
        u = _rms(h, n2_ref[...]).astype(BF16)
        ps = _dot(u, wps_ref[...])
        p_ref[rows, :] = ps[:, :D_POOL]
        s_ref[rows, :] = ps[:, D_POOL:]

    _trace_round_robin([chain(rows) for rows in _row_groups(x_ref.shape[0])], skew=FFN_SKEW)


def _ffn1_call(x, n1, wg, wu, wd, n2, wps):
    B, L, D = x.shape
    T = FFN_TILE
    tok = pl.BlockSpec((None, T, D), lambda b, i: (b, i, 0))
    return pl.pallas_call(
        _ffn1_kernel,
        grid=(B, L // T),
        in_specs=[
            tok,
            _const_spec((1, D)),
            _const_spec((D, D_FF)),
            _const_spec((D, D_FF)),
            _const_spec((D_FF, D)),
            _const_spec((1, D)),
            _const_spec((D, D_POOL + D_SSM)),
        ],
        out_specs=[
            tok,
            pl.BlockSpec((None, T, D_POOL), lambda b, i: (b, i, 0)),
            pl.BlockSpec((None, T, D_SSM), lambda b, i: (b, i, 0)),
        ],
        out_shape=[
            jax.ShapeDtypeStruct((B, L, D), F32),
            jax.ShapeDtypeStruct((B, L, D_POOL), F32),
            jax.ShapeDtypeStruct((B, L, D_SSM), F32),
        ],
        compiler_params=pltpu.CompilerParams(
            dimension_semantics=("arbitrary", "arbitrary"), vmem_limit_bytes=V7X_VMEM_LIMIT_BYTES),
        name="ffn1",
    )(x, n1, wg, wu, wd, n2, wps)


def _ffn2_kernel(h_ref, n_ref, wg_ref, wu_ref, wd_ref, fn_ref, o_ref):
    def chain(rows):
        h = yield from _swiglu_residual(h_ref[rows, :], n_ref[...], wg_ref, wu_ref, wd_ref)
        o_ref[rows, :] = _rms(h, fn_ref[...])

    _trace_round_robin([chain(rows) for rows in _row_groups(h_ref.shape[0])], skew=FFN_SKEW)


def _ffn2_call(h, n, wg, wu, wd, fn):
    B, L, D = h.shape
    T = FFN_TILE
    tok = pl.BlockSpec((None, T, D), lambda b, i: (b, i, 0))
    return pl.pallas_call(
        _ffn2_kernel,
        grid=(B, L // T),
        in_specs=[
            tok,
            _const_spec((1, D)),
            _const_spec((D, D_FF)),
            _const_spec((D, D_FF)),
            _const_spec((D_FF, D)),
            _const_spec((1, D)),
        ],
        out_specs=tok,
        out_shape=jax.ShapeDtypeStruct((B, L, D), F32),
        compiler_params=pltpu.CompilerParams(
            dimension_semantics=("arbitrary", "arbitrary"), vmem_limit_bytes=V7X_VMEM_LIMIT_BYTES),
        name="ffn2",
    )(h, n, wg, wu, wd, fn)


def _ssm_kernel(sf_ref, sb_ref, are_ref, aim_ref, ldt_ref, bre_ref, bim_ref, crt_ref, cit_ref, d_ref,
                yf_ref, yb_ref, abar_scr, bmat_scr, cmat_scr, state_scr, bu_scr, xb_scr):
    i = pl.program_id(0)
    S = D_STATE
    n_b, steps, _ = sf_ref.shape
    R = steps * n_b

    @pl.when(i == 0)
    def _():
        for d in range(2):
            ar = are_ref[d]
            ai = aim_ref[d]
            dt = jnp.exp(ldt_ref[d])
            mag = jnp.exp(dt * ar)
            ang = dt * ai
            abr = mag * jnp.cos(ang)
            abi = mag * jnp.sin(ang)
            den = ar * ar + ai * ai
            nr = abr - 1.0
            qr = (nr * ar + abi * ai) / den
            qi = (abi * ar - nr * ai) / den
            abar_scr[d, 0] = jnp.broadcast_to(abr, (8, S))
            abar_scr[d, 1] = jnp.broadcast_to(abi, (8, S))
            br = bre_ref[d]
            bi = bim_ref[d]
            bmat_scr[d, :, :S] = (qr * br - qi * bi).astype(BF16)
            bmat_scr[d, :, S:] = (qr * bi + qi * br).astype(BF16)
            cmat_scr[d, :S, :] = crt_ref[d]
            cmat_scr[d, S:, :] = -cit_ref[d]
        state_scr[...] = jnp.zeros_like(state_scr)

    sub = SSM_SUBTILE
    n_sub = steps // sub
    halves = [(slice(c0, c0 + SCAN_LANES), slice(S + c0, S + c0 + SCAN_LANES))
              for c0 in range(0, S, SCAN_LANES)]
    s_refs = (sf_ref, sb_ref)
    order = (list(range(n_sub)), list(range(n_sub - 1, -1, -1)))

    for d in range(2):
        for j in order[d]:
            s_nat = s_refs[d][:, j * sub:(j + 1) * sub, :]
            s_tb = jnp.swapaxes(s_nat, 0, 1).reshape(sub * n_b, D_SSM)
            bu_scr[d, j * sub * n_b:(j + 1) * sub * n_b, :] = _dot(s_tb.astype(BF16), bmat_scr[d])

    for d, y_ref in ((0, yf_ref), (1, yb_ref)):
        coef = [(abar_scr[d, 0, :, re], abar_scr[d, 1, :, re]) for re, _ in halves]
        state = [(state_scr[d, 0, :, re], state_scr[d, 1, :, re]) for re, _ in halves]
        for j in order[d]:
            for hf, (re, im) in enumerate(halves):
                ar, ai = coef[hf]
                xr, xi = state[hf]
                for k in range(0, sub, 2):
                    lo = j * sub + (k if d == 0 else sub - 2 - k)
                    got = {}
                    for t in ((lo, lo + 1) if d == 0 else (lo + 1, lo)):
                        rows = slice(n_b * t, n_b * (t + 1))
                        nxr = ar * xr - ai * xi + bu_scr[d, rows, re]
                        nxi = ar * xi + ai * xr + bu_scr[d, rows, im]
                        xr, xi = nxr, nxi
                        got[t] = (nxr, nxi)
                    rows2 = slice(n_b * lo, n_b * (lo + 2))
                    xb_scr[d, rows2, re] = jnp.concatenate([got[lo][0], got[lo + 1][0]], axis=0).astype(BF16)
                    xb_scr[d, rows2, im] = jnp.concatenate([got[lo][1], got[lo + 1][1]], axis=0).astype(BF16)
                state[hf] = (xr, xi)
            rows_j = slice(j * sub * n_b, (j + 1) * sub * n_b)
            y_tb = _dot(xb_scr[d, rows_j, :], cmat_scr[d])
            y = jnp.swapaxes(y_tb.reshape(sub, n_b, D_SSM), 0, 1)
            if d == 0:
                y = y + sf_ref[:, j * sub:(j + 1) * sub, :] * d_ref[...]
            y_ref[:, j * sub:(j + 1) * sub, :] = y
        for hf, (re, _) in enumerate(halves):
            state_scr[d, 0, :, re] = state[hf][0]
            state_scr[d, 1, :, re] = state[hf][1]


def _ssm_call(s, a_re, a_im, log_dt, b_re_bd, b_im_bd, c_re_t, c_im_t, d_skip):
    B, L, _ = s.shape
    assert B == V7X_SUBLANES, "the scan keeps one batch element per sublane"
    T = SSM_TILE
    n = L // T
    S = D_STATE
    fwd = pl.BlockSpec((B, T, D_SSM), lambda i: (0, i, 0))
    bwd = pl.BlockSpec((B, T, D_SSM), lambda i: (0, n - 1 - i, 0))
    return pl.pallas_call(
        _ssm_kernel,
        grid=(n,),
        in_specs=[
            fwd, bwd,
            _const_spec((2, 1, S)), _const_spec((2, 1, S)), _const_spec((2, 1, S)),
            _const_spec((2, D_SSM, S)), _const_spec((2, D_SSM, S)),
            _const_spec((2, S, D_SSM)), _const_spec((2, S, D_SSM)),
            _const_spec((1, D_SSM)),
        ],
        out_specs=[fwd, bwd],
        out_shape=[jax.ShapeDtypeStruct((B, L, D_SSM), F32)] * 2,
        scratch_shapes=[
            pltpu.VMEM((2, 2, 8, S), F32),
            pltpu.VMEM((2, D_SSM, 2 * S), BF16),
            pltpu.VMEM((2, 2 * S, D_SSM), BF16),
            pltpu.VMEM((2, 2, 8, S), F32),
            pltpu.VMEM((2, B * T, 2 * S), F32),
            pltpu.VMEM((2, B * T, 2 * S), BF16),
        ],
        compiler_params=pltpu.CompilerParams(
            dimension_semantics=("arbitrary",), vmem_limit_bytes=V7X_VMEM_LIMIT_BYTES),
        name="ssm_scan",
    )(s, s, a_re, a_im, log_dt, b_re_bd, b_im_bd, c_re_t, c_im_t, d_skip)


def _window_sum(v, w):
    n = v.shape[0]
    ahead = lambda a, k: pltpu.roll(a, n - k, axis=0)
    behind = lambda a, k: pltpu.roll(a, k, axis=0)
    if w == 2:
        return v + behind(v, 1)
    cur, span = v + ahead(v, 1), 2
    while span < w // 2:
        cur, span = cur + ahead(cur, span), 2 * span
    return cur + behind(cur, span)


def _mix_kernel(h_ref, p_ref, pl_ref, pr_ref, yf_ref, yb_ref, k_ref, v_ref,
                nmix_ref, wgate_ref, poolw_ref, pscale_ref, wpp_ref, wglu_ref, wmo_ref,
                nx_ref, wq_ref, wxo_ref, o_ref, *, seq_len):
    i = pl.program_id(1)
    n_i = pl.num_programs(1)
    T = h_ref.shape[0]
    H = POOL_HALO

    halo_l = jnp.where(i > 0, pl_ref[...], 0.0)
    halo_r = jnp.where(i < n_i - 1, pr_ref[...], 0.0)
    edge = V7X_SUBLANES
    t_edge = lax.broadcasted_iota(jnp.int32, (edge, POOL_GROUP), 0)
    pooled_groups = []
    for g, w in enumerate(POOL_WINDOWS):
        left = w // 2
        right = w - 1 - left
        lanes = slice(g * POOL_GROUP, (g + 1) * POOL_GROUP)
        v = jnp.concatenate([halo_l[:, lanes], p_ref[:, lanes], halo_r[:, lanes]], axis=0)
        tot = _window_sum(v, w)[H:H + T]
        self_ = v[H:H + T]

        def clipped(t0, rows):
            t = i * T + t0 + t_edge
            cnt = jnp.minimum(t + (right + 1), seq_len) - jnp.maximum(t - left, 0)
            return tot[rows] / cnt.astype(F32) - self_[rows]

        pooled_groups.append(jnp.concatenate([
            clipped(0, slice(0, edge)),
            tot[edge:T - edge] * (1.0 / w) - self_[edge:T - edge],
            clipped(T - edge, slice(T - edge, T)),
        ], axis=0).astype(BF16))

    def chain(rows):
        h = h_ref[rows, :]
        mixed = [_dot(pooled_groups[g][rows], poolw_ref[g]) for g in range(len(POOL_WINDOWS))]
        yield
        mixed = jnp.concatenate(mixed, axis=-1) * pscale_ref[...]
        z_pool = _dot(mixed.astype(BF16), wpp_ref[...])
        yield

        y = jax.nn.gelu(yf_ref[rows, :] + yb_ref[rows, :]).astype(BF16)
        glu = _dot(y, wglu_ref[...])
        yield
        z_ssm = glu[:, :D_MODEL] * jax.nn.sigmoid(glu[:, D_MODEL:])

        u = _rms(h, nmix_ref[...]).astype(BF16)
        gates = _dot(u, wgate_ref[...])
        yield
        merged = (jax.nn.sigmoid(gates[:, :D_MODEL]) * z_pool
                  + jax.nn.sigmoid(gates[:, D_MODEL:]) * z_ssm)
        h = h + _dot(merged.astype(BF16), wmo_ref[...])
        yield

        q = _dot(_rms(h, nx_ref[...]).astype(BF16), wq_ref[...])
        yield
        head_cols = [slice(hd * XHEAD_DIM, (hd + 1) * XHEAD_DIM) for hd in range(N_XHEADS)]
        scores = [lax.dot_general(q[:, cols].astype(BF16), k_ref[:, cols], (((1,), (1,)), ((), ())),
                                  preferred_element_type=F32) / math.sqrt(XHEAD_DIM) for cols in head_cols]
        yield
        heads = []
        for sc, cols in zip(scores, head_cols):
            e = jnp.exp(sc - jnp.max(sc, axis=-1, keepdims=True))
            probs = e / jnp.sum(e, axis=-1, keepdims=True)
            heads.append(_dot(probs.astype(BF16), v_ref[:, cols]))
        yield
        o = jnp.concatenate(heads, axis=-1).astype(BF16)
        o_ref[rows, :] = h + _dot(o, wxo_ref[...])

    _trace_round_robin([chain(rows) for rows in _row_groups(T)], skew=MIX_SKEW)


def _mix_call(h, p, yf, yb, k, v, nmix, wgate, poolw, pscale, wpp, wglu, wmo, nx, wq, wxo):
    B, L, D = h.shape
    T = MIX_TILE
    H = POOL_HALO
    per_tile = T // H
    n_halo = L // H
    tok = pl.BlockSpec((None, T, D), lambda b, i: (b, i, 0))
    y_spec = pl.BlockSpec((None, T, D_SSM), lambda b, i: (b, i, 0))
    mem_spec = pl.BlockSpec((None, N_MEM, D), lambda b, i: (b, 0, 0))
    return pl.pallas_call(
        functools.partial(_mix_kernel, seq_len=L),
        grid=(B, L // T),
        in_specs=[
            tok,
            pl.BlockSpec((None, T, D_POOL), lambda b, i: (b, i, 0)),
            pl.BlockSpec((None, H, D_POOL), lambda b, i: (b, jnp.maximum(i * per_tile - 1, 0), 0)),
            pl.BlockSpec((None, H, D_POOL), lambda b, i: (b, jnp.minimum((i + 1) * per_tile, n_halo - 1), 0)),
            y_spec, y_spec,
            mem_spec, mem_spec,
            _const_spec((1, D)),
            _const_spec((D, 2 * D)),
            _const_spec((len(POOL_WINDOWS), POOL_GROUP, POOL_GROUP)),
            _const_spec((1, D_POOL)),
            _const_spec((D_POOL, D)),
            _const_spec((D_SSM, 2 * D)),
            _const_spec((D, D)),
            _const_spec((1, D)),
            _const_spec((D, D)),
            _const_spec((D, D)),
        ],
        out_specs=tok,
        out_shape=jax.ShapeDtypeStruct((B, L, D), F32),
        compiler_params=pltpu.CompilerParams(
            dimension_semantics=("arbitrary", "arbitrary"), vmem_limit_bytes=V7X_VMEM_LIMIT_BYTES),
        name="mix_attn",
    )(h, p, p, p, yf, yb, k, v, nmix, wgate, poolw, pscale, wpp, wglu, wmo, nx, wq, wxo)


def _block_diag_in(w):
    G, P, Hh = w.shape
    eye = jnp.eye(G, dtype=w.dtype)
    return jnp.einsum("gph,gk->ghkp", w, eye).reshape(G * Hh, G * P)


def _block_diag_out(w):
    G, Hh, P = w.shape
    eye = jnp.eye(G, dtype=w.dtype)
    return jnp.einsum("ghp,gk->gpkh", w, eye).reshape(G * P, G * Hh)


def kernel(x, mem, ffn1_norm, ffn1_w_gate, ffn1_w_up, ffn1_w_down, mix_norm, w_in, pool_w, pool_scale,
           w_pool_proj, ssm_a_re, ssm_a_im, ssm_log_dt, ssm_b_re, ssm_b_im, ssm_c_re, ssm_c_im, ssm_d,
           w_glu_val, w_glu_gate, w_mix_out, xattn_norm, mem_norm, w_q, w_kv, w_xo, ffn2_norm,
           ffn2_w_gate, ffn2_w_up, ffn2_w_down, final_norm):
    B, L, D = x.shape
    depth = ffn1_norm.shape[0]
    row = lambda v: v.reshape(1, -1)
    bf = lambda w: w.astype(BF16)
    h = x
    for l in range(depth):
        k_mem, v_mem = _kv_call(mem, row(mem_norm[l]), bf(w_kv[l]))

        h, p, s = _ffn1_call(
            h, row(ffn1_norm[l]), bf(ffn1_w_gate[l]), bf(ffn1_w_up[l]), bf(ffn1_w_down[l]),
            row(mix_norm[l]), bf(w_in[l][:, :D_POOL + D_SSM]))

        G, P = N_SSM_GROUPS, SSM_STATE
        per_state = lambda a: a.reshape(2, 1, G * P)
        log_dt = jnp.broadcast_to(ssm_log_dt[l][:, :, None], (2, G, P))
        yf, yb = _ssm_call(
            s, per_state(ssm_a_re[l]), per_state(ssm_a_im[l]), per_state(log_dt),
            jax.vmap(_block_diag_in)(ssm_b_re[l]), jax.vmap(_block_diag_in)(ssm_b_im[l]),
            bf(jax.vmap(_block_diag_out)(ssm_c_re[l])), bf(jax.vmap(_block_diag_out)(ssm_c_im[l])),
            row(ssm_d[l]))

        h = _mix_call(
            h, p, yf, yb, k_mem, v_mem,
            row(mix_norm[l]), bf(w_in[l][:, D_POOL + D_SSM:]), bf(pool_w[l]), row(pool_scale[l]),
            bf(w_pool_proj[l]), bf(jnp.concatenate([w_glu_val[l], w_glu_gate[l]], axis=1)),
            bf(w_mix_out[l]), row(xattn_norm[l]), bf(w_q[l]), bf(w_xo[l]))

        last = l == depth - 1
        assert last, "deeper stacks need a non-final variant of the second SwiGLU kernel"
        h = _ffn2_call(h, row(ffn2_norm[l]), bf(ffn2_w_gate[l]), bf(ffn2_w_up[l]), bf(ffn2_w_down[l]),
                       row(final_norm))
    return h
```

```python
import functools
import math

import jax
import jax.numpy as jnp
from jax import lax
from jax.experimental import pallas as pl
from jax.experimental.pallas import tpu as pltpu

F32 = jnp.float32
BF16 = jnp.bfloat16

D_MODEL = 1024
N_MEM = 256
D_FF = 2816
D_POOL = 512
POOL_WINDOWS = (2, 4, 8, 16)
POOL_GROUP = 128
D_SSM = 256
SSM_GROUP = 16
N_SSM_GROUPS = 16
SSM_STATE = 64
D_STATE = N_SSM_GROUPS * SSM_STATE
N_XHEADS = 4
XHEAD_DIM = 256
EPS = 1e-6

V7X_VMEM_LIMIT_BYTES = 56 * 1024 * 1024
V7X_SUBLANES = 8
POOL_HALO = 16

FFN_TILE = 1024
MIX_TILE = 1024
SSM_TILE = 128
FFN_SKEW = 1
MIX_SKEW = 1
GROUP_ROWS = 256
FF_CHUNKS = ((0, 768), (768, 1536), (1536, 2304), (2304, 2816))
SSM_SUBTILE = 32
SCAN_LANES = 512


def _rms(v, g):
    r = lax.rsqrt(jnp.mean(v * v, axis=-1, keepdims=True) + EPS)
    return (v * r) * g


def _dot(a, b):
    return jnp.dot(a, b, preferred_element_type=F32)


def _const_spec(shape):
    nd = len(shape)
    return pl.BlockSpec(shape, lambda *_: (0,) * nd)


def _kv_kernel(mem_ref, g_ref, wkv_ref, k_ref, v_ref):
    m = _rms(mem_ref[...], g_ref[...]).astype(BF16)
    kv = _dot(m, wkv_ref[...])
    k_ref[...] = kv[:, :D_MODEL].astype(BF16)
    v_ref[...] = kv[:, D_MODEL:].astype(BF16)


def _kv_call(mem, mem_norm, w_kv):
    B = mem.shape[0]
    return pl.pallas_call(
        _kv_kernel,
        grid=(B,),
        in_specs=[
            pl.BlockSpec((None, N_MEM, D_MODEL), lambda b: (b, 0, 0)),
            _const_spec((1, D_MODEL)),
            _const_spec((D_MODEL, 2 * D_MODEL)),
        ],
        out_specs=[
            pl.BlockSpec((None, N_MEM, D_MODEL), lambda b: (b, 0, 0)),
            pl.BlockSpec((None, N_MEM, D_MODEL), lambda b: (b, 0, 0)),
        ],
        out_shape=[jax.ShapeDtypeStruct((B, N_MEM, D_MODEL), BF16)] * 2,
        compiler_params=pltpu.CompilerParams(
            dimension_semantics=("arbitrary",), vmem_limit_bytes=V7X_VMEM_LIMIT_BYTES),
        name="kv_proj",
    )(mem, mem_norm, w_kv)


def _row_groups(n_rows):
    return [slice(r, r + GROUP_ROWS) for r in range(0, n_rows, GROUP_ROWS)]


def _trace_round_robin(chains, skew=0):
    waiting = list(chains)
    n_started = 0
    live = []
    turn = 0
    while waiting or live:
        while waiting and turn >= n_started * skew:
            live.append(waiting.pop(0))
            n_started += 1
        for c in list(live):
            try:
                next(c)
            except StopIteration:
                live.remove(c)
        turn += 1


def _swiglu_residual(x, g, wg_ref, wu_ref, wd_ref):
    u = _rms(x, g).astype(BF16)
    acc = None
    for c0, c1 in FF_CHUNKS:
        gate = _dot(u, wg_ref[:, c0:c1])
        up = _dot(u, wu_ref[:, c0:c1])
        yield
        act = (gate * jax.nn.sigmoid(gate) * up).astype(BF16)
        part = _dot(act, wd_ref[c0:c1, :])
        acc = part if acc is None else acc + part
        yield
    return x + 0.5 * acc


def _ffn1_kernel(x_ref, n1_ref, wg_ref, wu_ref, wd_ref, n2_ref, wps_ref, h_ref, p_ref, s_ref):
    def chain(rows):
        h = yield from _swiglu_residual(x_ref[rows, :], n1_ref[...], wg_ref, wu_ref, wd_ref)
        h_ref[rows, :] = h
        u = _rms(h, n2_ref[...]).astype(BF16)
        ps = _dot(u, wps_ref[...])
        p_ref[rows, :] = ps[:, :D_POOL]
        s_ref[rows, :] = ps[:, D_POOL:]

    _trace_round_robin([chain(rows) for rows in _row_groups(x_ref.shape[0])], skew=FFN_SKEW)


def _ffn1_call(x, n1, wg, wu, wd, n2, wps):
    B, L, D = x.shape
    T = FFN_TILE
    tok = pl.BlockSpec((None, T, D), lambda b, i: (b, i, 0))
    return pl.pallas_call(
        _ffn1_kernel,
        grid=(B, L // T),
        in_specs=[
            tok,
            _const_spec((1, D)),
            _const_spec((D, D_FF)),
            _const_spec((D, D_FF)),
            _const_spec((D_FF, D)),
            _const_spec((1, D)),
            _const_spec((D, D_POOL + D_SSM)),
        ],
        out_specs=[
            tok,
            pl.BlockSpec((None, T, D_POOL), lambda b, i: (b, i, 0)),
            pl.BlockSpec((None, T, D_SSM), lambda b, i: (b, i, 0)),
        ],
        out_shape=[
            jax.ShapeDtypeStruct((B, L, D), F32),
            jax.ShapeDtypeStruct((B, L, D_POOL), F32),
            jax.ShapeDtypeStruct((B, L, D_SSM), F32),
        ],
        compiler_params=pltpu.CompilerParams(
            dimension_semantics=("arbitrary", "arbitrary"), vmem_limit_bytes=V7X_VMEM_LIMIT_BYTES),
        name="ffn1",
    )(x, n1, wg, wu, wd, n2, wps)


def _ffn2_kernel(h_ref, n_ref, wg_ref, wu_ref, wd_ref, fn_ref, o_ref):
    def chain(rows):
        h = yield from _swiglu_residual(h_ref[rows, :], n_ref[...], wg_ref, wu_ref, wd_ref)
        o_ref[rows, :] = _rms(h, fn_ref[...])

    _trace_round_robin([chain(rows) for rows in _row_groups(h_ref.shape[0])], skew=FFN_SKEW)


def _ffn2_call(h, n, wg, wu, wd, fn):
    B, L, D = h.shape
    T = FFN_TILE
    tok = pl.BlockSpec((None, T, D), lambda b, i: (b, i, 0))
    return pl.pallas_call(
        _ffn2_kernel,
        grid=(B, L // T),
        in_specs=[
            tok,
            _const_spec((1, D)),
            _const_spec((D, D_FF)),
            _const_spec((D, D_FF)),
            _const_spec((D_FF, D)),
            _const_spec((1, D)),
        ],
        out_specs=tok,
        out_shape=jax.ShapeDtypeStruct((B, L, D), F32),
        compiler_params=pltpu.CompilerParams(
            dimension_semantics=("arbitrary", "arbitrary"), vmem_limit_bytes=V7X_VMEM_LIMIT_BYTES),
        name="ffn2",
    )(h, n, wg, wu, wd, fn)


def _ssm_kernel(sf_ref, sb_ref, are_ref, aim_ref, ldt_ref, bre_ref, bim_ref, crt_ref, cit_ref, d_ref,
                yf_ref, yb_ref, abar_scr, bmat_scr, cmat_scr, state_scr, bu_scr, xb_scr):
    i = pl.program_id(0)
    S = D_STATE
    n_b, steps, _ = sf_ref.shape
    R = steps * n_b

    @pl.when(i == 0)
    def _():
        for d in range(2):
            ar = are_ref[d]
            ai = aim_ref[d]
            dt = jnp.exp(ldt_ref[d])
            mag = jnp.exp(dt * ar)
            ang = dt * ai
            abr = mag * jnp.cos(ang)
            abi = mag * jnp.sin(ang)
            den = ar * ar + ai * ai
            nr = abr - 1.0
            qr = (nr * ar + abi * ai) / den
            qi = (abi * ar - nr * ai) / den
            abar_scr[d, 0] = jnp.broadcast_to(abr, (8, S))
            abar_scr[d, 1] = jnp.broadcast_to(abi, (8, S))
            br = bre_ref[d]
            bi = bim_ref[d]
            bmat_scr[d, :, :S] = (qr * br - qi * bi).astype(BF16)
            bmat_scr[d, :, S:] = (qr * bi + qi * br).astype(BF16)
            cmat_scr[d, :S, :] = crt_ref[d]
            cmat_scr[d, S:, :] = -cit_ref[d]
        state_scr[...] = jnp.zeros_like(state_scr)

    sub = SSM_SUBTILE
    n_sub = steps // sub
    halves = [(slice(c0, c0 + SCAN_LANES), slice(S + c0, S + c0 + SCAN_LANES))
              for c0 in range(0, S, SCAN_LANES)]
    def chain(d, s_ref, y_ref):
        order = list(range(n_sub)) if d == 0 else list(range(n_sub - 1, -1, -1))

        def input_map(j):
            s_nat = s_ref[:, j * sub:(j + 1) * sub, :]
            s_tb = jnp.swapaxes(s_nat, 0, 1).reshape(sub * n_b, D_SSM)
            bu_scr[d, j * sub * n_b:(j + 1) * sub * n_b, :] = _dot(s_tb.astype(BF16), bmat_scr[d])

        coef = [(abar_scr[d, 0, :, re], abar_scr[d, 1, :, re]) for re, _ in halves]
        state = [(state_scr[d, 0, :, re], state_scr[d, 1, :, re]) for re, _ in halves]
        input_map(order[0])
        yield
        for pos, j in enumerate(order):
            if pos + 1 < n_sub:
                input_map(order[pos + 1])
            for hf, (re, im) in enumerate(halves):
                ar, ai = coef[hf]
                xr, xi = state[hf]
                for k in range(0, sub, 2):
                    lo = j * sub + (k if d == 0 else sub - 2 - k)
                    got = {}
                    for t in ((lo, lo + 1) if d == 0 else (lo + 1, lo)):
                        rows = slice(n_b * t, n_b * (t + 1))
                        nxr = ar * xr - ai * xi + bu_scr[d, rows, re]
                        nxi = ar * xi + ai * xr + bu_scr[d, rows, im]
                        xr, xi = nxr, nxi
                        got[t] = (nxr, nxi)
                    rows2 = slice(n_b * lo, n_b * (lo + 2))
                    xb_scr[d, rows2, re] = jnp.concatenate([got[lo][0], got[lo + 1][0]], axis=0).astype(BF16)
                    xb_scr[d, rows2, im] = jnp.concatenate([got[lo][1], got[lo + 1][1]], axis=0).astype(BF16)
                state[hf] = (xr, xi)
            rows_j = slice(j * sub * n_b, (j + 1) * sub * n_b)
            y_tb = _dot(xb_scr[d, rows_j, :], cmat_scr[d])
            y = jnp.swapaxes(y_tb.reshape(sub, n_b, D_SSM), 0, 1)
            if d == 0:
                y = y + sf_ref[:, j * sub:(j + 1) * sub, :] * d_ref[...]
            y_ref[:, j * sub:(j + 1) * sub, :] = y
            yield
        for hf, (re, _) in enumerate(halves):
            state_scr[d, 0, :, re] = state[hf][0]
            state_scr[d, 1, :, re] = state[hf][1]

    _trace_round_robin([chain(0, sf_ref, yf_ref), chain(1, sb_ref, yb_ref)])


def _ssm_call(s, a_re, a_im, log_dt, b_re_bd, b_im_bd, c_re_t, c_im_t, d_skip):
    B, L, _ = s.shape
    assert B == V7X_SUBLANES, "the scan keeps one batch element per sublane"
    T = SSM_TILE
    n = L // T
    S = D_STATE
    fwd = pl.BlockSpec((B, T, D_SSM), lambda i: (0, i, 0))
    bwd = pl.BlockSpec((B, T, D_SSM), lambda i: (0, n - 1 - i, 0))
    return pl.pallas_call(
        _ssm_kernel,
        grid=(n,),
        in_specs=[
            fwd, bwd,
            _const_spec((2, 1, S)), _const_spec((2, 1, S)), _const_spec((2, 1, S)),
            _const_spec((2, D_SSM, S)), _const_spec((2, D_SSM, S)),
            _const_spec((2, S, D_SSM)), _const_spec((2, S, D_SSM)),
            _const_spec((1, D_SSM)),
        ],
        out_specs=[fwd, bwd],
        out_shape=[jax.ShapeDtypeStruct((B, L, D_SSM), F32)] * 2,
        scratch_shapes=[
            pltpu.VMEM((2, 2, 8, S), F32),
            pltpu.VMEM((2, D_SSM, 2 * S), BF16),
            pltpu.VMEM((2, 2 * S, D_SSM), BF16),
            pltpu.VMEM((2, 2, 8, S), F32),
            pltpu.VMEM((2, B * T, 2 * S), F32),
            pltpu.VMEM((2, B * T, 2 * S), BF16),
        ],
        compiler_params=pltpu.CompilerParams(
            dimension_semantics=("arbitrary",), vmem_limit_bytes=V7X_VMEM_LIMIT_BYTES),
        name="ssm_scan",
    )(s, s, a_re, a_im, log_dt, b_re_bd, b_im_bd, c_re_t, c_im_t, d_skip)


def _window_sum(v, w):
    n = v.shape[0]
    ahead = lambda a, k: pltpu.roll(a, n - k, axis=0)
    behind = lambda a, k: pltpu.roll(a, k, axis=0)
    if w == 2:
        return v + behind(v, 1)
    cur, span = v + ahead(v, 1), 2
    while span < w // 2:
        cur, span = cur + ahead(cur, span), 2 * span
    return cur + behind(cur, span)


def _mix_kernel(h_ref, p_ref, pl_ref, pr_ref, yf_ref, yb_ref, k_ref, v_ref,
                nmix_ref, wgate_ref, poolw_ref, pscale_ref, wpp_ref, wglu_ref, wmo_ref,
                nx_ref, wq_ref, wxo_ref, o_ref, *, seq_len):
    i = pl.program_id(1)
    n_i = pl.num_programs(1)
    T = h_ref.shape[0]
    H = POOL_HALO

    halo_l = jnp.where(i > 0, pl_ref[...], 0.0)
    halo_r = jnp.where(i < n_i - 1, pr_ref[...], 0.0)
    edge = V7X_SUBLANES
    t_edge = lax.broadcasted_iota(jnp.int32, (edge, POOL_GROUP), 0)
    pooled_groups = []
    for g, w in enumerate(POOL_WINDOWS):
        left = w // 2
        right = w - 1 - left
        lanes = slice(g * POOL_GROUP, (g + 1) * POOL_GROUP)
        v = jnp.concatenate([halo_l[:, lanes], p_ref[:, lanes], halo_r[:, lanes]], axis=0)
        tot = _window_sum(v, w)[H:H + T]
        self_ = v[H:H + T]

        def clipped(t0, rows):
            t = i * T + t0 + t_edge
            cnt = jnp.minimum(t + (right + 1), seq_len) - jnp.maximum(t - left, 0)
            return tot[rows] / cnt.astype(F32) - self_[rows]

        pooled_groups.append(jnp.concatenate([
            clipped(0, slice(0, edge)),
            tot[edge:T - edge] * (1.0 / w) - self_[edge:T - edge],
            clipped(T - edge, slice(T - edge, T)),
        ], axis=0).astype(BF16))

    def chain(rows):
        h = h_ref[rows, :]
        mixed = [_dot(pooled_groups[g][rows], poolw_ref[g]) for g in range(len(POOL_WINDOWS))]
        yield
        mixed = jnp.concatenate(mixed, axis=-1) * pscale_ref[...]
        z_pool = _dot(mixed.astype(BF16), wpp_ref[...])
        yield

        y = jax.nn.gelu(yf_ref[rows, :] + yb_ref[rows, :]).astype(BF16)
        glu = _dot(y, wglu_ref[...])
        yield
        z_ssm = glu[:, :D_MODEL] * jax.nn.sigmoid(glu[:, D_MODEL:])

        u = _rms(h, nmix_ref[...]).astype(BF16)
        gates = _dot(u, wgate_ref[...])
        yield
        merged = (jax.nn.sigmoid(gates[:, :D_MODEL]) * z_pool
                  + jax.nn.sigmoid(gates[:, D_MODEL:]) * z_ssm)
        h = h + _dot(merged.astype(BF16), wmo_ref[...])
        yield

        q = _dot(_rms(h, nx_ref[...]).astype(BF16), wq_ref[...])
        yield
        head_cols = [slice(hd * XHEAD_DIM, (hd + 1) * XHEAD_DIM) for hd in range(N_XHEADS)]
        scores = [lax.dot_general(q[:, cols].astype(BF16), k_ref[:, cols], (((1,), (1,)), ((), ())),
                                  preferred_element_type=F32) / math.sqrt(XHEAD_DIM) for cols in head_cols]
        yield
        heads = []
        for sc, cols in zip(scores, head_cols):
            e = jnp.exp(sc - jnp.max(sc, axis=-1, keepdims=True))
            probs = e / jnp.sum(e, axis=-1, keepdims=True)
            heads.append(_dot(probs.astype(BF16), v_ref[:, cols]))
        yield
        o = jnp.concatenate(heads, axis=-1).astype(BF16)
        o_ref[rows, :] = h + _dot(o, wxo_ref[...])

    _trace_round_robin([chain(rows) for rows in _row_groups(T)], skew=MIX_SKEW)


def _mix_call(h, p, yf, yb, k, v, nmix, wgate, poolw, pscale, wpp, wglu, wmo, nx, wq, wxo):
    B, L, D = h.shape
    T = MIX_TILE
    H = POOL_HALO
    per_tile = T // H
    n_halo = L // H
    tok = pl.BlockSpec((None, T, D), lambda b, i: (b, i, 0))
    y_spec = pl.BlockSpec((None, T, D_SSM), lambda b, i: (b, i, 0))
    mem_spec = pl.BlockSpec((None, N_MEM, D), lambda b, i: (b, 0, 0))
    return pl.pallas_call(
        functools.partial(_mix_kernel, seq_len=L),
        grid=(B, L // T),
        in_specs=[
            tok,
            pl.BlockSpec((None, T, D_POOL), lambda b, i: (b, i, 0)),
            pl.BlockSpec((None, H, D_POOL), lambda b, i: (b, jnp.maximum(i * per_tile - 1, 0), 0)),
            pl.BlockSpec((None, H, D_POOL), lambda b, i: (b, jnp.minimum((i + 1) * per_tile, n_halo - 1), 0)),
            y_spec, y_spec,
            mem_spec, mem_spec,
            _const_spec((1, D)),
            _const_spec((D, 2 * D)),
            _const_spec((len(POOL_WINDOWS), POOL_GROUP, POOL_GROUP)),
            _const_spec((1, D_POOL)),
            _const_spec((D_POOL, D)),
            _const_spec((D_SSM, 2 * D)),
            _const_spec((D, D)),
            _const_spec((1, D)),
            _const_spec((D, D)),
            _const_spec((D, D)),
        ],
        out_specs=tok,
        out_shape=jax.ShapeDtypeStruct((B, L, D), F32),
        compiler_params=pltpu.CompilerParams(
            dimension_semantics=("arbitrary", "arbitrary"), vmem_limit_bytes=V7X_VMEM_LIMIT_BYTES),
        name="mix_attn",
    )(h, p, p, p, yf, yb, k, v, nmix, wgate, poolw, pscale, wpp, wglu, wmo, nx, wq, wxo)


def _block_diag_in(w):
    G, P, Hh = w.shape
    eye = jnp.eye(G, dtype=w.dtype)
    return jnp.einsum("gph,gk->ghkp", w, eye).reshape(G * Hh, G * P)


def _block_diag_out(w):
    G, Hh, P = w.shape
    eye = jnp.eye(G, dtype=w.dtype)
    return jnp.einsum("ghp,gk->gpkh", w, eye).reshape(G * P, G * Hh)


def kernel(x, mem, ffn1_norm, ffn1_w_gate, ffn1_w_up, ffn1_w_down, mix_norm, w_in, pool_w, pool_scale,
           w_pool_proj, ssm_a_re, ssm_a_im, ssm_log_dt, ssm_b_re, ssm_b_im, ssm_c_re, ssm_c_im, ssm_d,
           w_glu_val, w_glu_gate, w_mix_out, xattn_norm, mem_norm, w_q, w_kv, w_xo, ffn2_norm,
           ffn2_w_gate, ffn2_w_up, ffn2_w_down, final_norm):
    B, L, D = x.shape
    depth = ffn1_norm.shape[0]
    row = lambda v: v.reshape(1, -1)
    bf = lambda w: w.astype(BF16)
    h = x
    for l in range(depth):
        k_mem, v_mem = _kv_call(mem, row(mem_norm[l]), bf(w_kv[l]))

        h, p, s = _ffn1_call(
            h, row(ffn1_norm[l]), bf(ffn1_w_gate[l]), bf(ffn1_w_up[l]), bf(ffn1_w_down[l]),
            row(mix_norm[l]), bf(w_in[l][:, :D_POOL + D_SSM]))

        G, P = N_SSM_GROUPS, SSM_STATE
        per_state = lambda a: a.reshape(2, 1, G * P)
        log_dt = jnp.broadcast_to(ssm_log_dt[l][:, :, None], (2, G, P))
        yf, yb = _ssm_call(
            s, per_state(ssm_a_re[l]), per_state(ssm_a_im[l]), per_state(log_dt),
            jax.vmap(_block_diag_in)(ssm_b_re[l]), jax.vmap(_block_diag_in)(ssm_b_im[l]),
            bf(jax.vmap(_block_diag_out)(ssm_c_re[l])), bf(jax.vmap(_block_diag_out)(ssm_c_im[l])),
            row(ssm_d[l]))

        h = _mix_call(
            h, p, yf, yb, k_mem, v_mem,
            row(mix_norm[l]), bf(w_in[l][:, D_POOL + D_SSM:]), bf(pool_w[l]), row(pool_scale[l]),
            bf(w_pool_proj[l]), bf(jnp.concatenate([w_glu_val[l], w_glu_gate[l]], axis=1)),
            bf(w_mix_out[l]), row(xattn_norm[l]), bf(w_q[l]), bf(w_xo[l]))

        last = l == depth - 1
        assert last, "deeper stacks need a non-final variant of the second SwiGLU kernel"
        h = _ffn2_call(h, row(ffn2_norm[l]), bf(ffn2_w_gate[l]), bf(ffn2_w_up[l]), bf(ffn2_w_down[l]),
                       row(final_norm))
    return h
```

```python
import functools
import math

import jax
import jax.numpy as jnp
from jax import lax
from jax.experimental import pallas as pl
from jax.experimental.pallas import tpu as pltpu

F32 = jnp.float32
BF16 = jnp.bfloat16

D_MODEL = 1024
N_MEM = 256
D_FF = 2816
D_POOL = 512
POOL_WINDOWS = (2, 4, 8, 16)
POOL_GROUP = 128
D_SSM = 256
SSM_GROUP = 16
N_SSM_GROUPS = 16
SSM_STATE = 64
D_STATE = N_SSM_GROUPS * SSM_STATE
N_XHEADS = 4
XHEAD_DIM = 256
EPS = 1e-6

V7X_VMEM_LIMIT_BYTES = 56 * 1024 * 1024
V7X_SUBLANES = 8
POOL_HALO = 16

FFN_TILE = 1024
MIX_TILE = 1024
SSM_TILE = 128
FFN_SKEW = 1
MIX_SKEW = 1
WEIGHT_CHUNKS = 8
GROUP_ROWS = 256
FF_CHUNKS = ((0, 768), (768, 1536), (1536, 2304), (2304, 2816))
SSM_SUBTILE = 32
SCAN_LANES = 512


def _rms(v, g):
    r = lax.rsqrt(jnp.mean(v * v, axis=-1, keepdims=True) + EPS)
    return (v * r) * g


def _dot(a, b):
    return jnp.dot(a, b, preferred_element_type=F32)


def _const_spec(shape):
    nd = len(shape)
    return pl.BlockSpec(shape, lambda *_: (0,) * nd)


def _kv_kernel(mem_ref, g_ref, wkv_ref, k_ref, v_ref):
    m = _rms(mem_ref[...], g_ref[...]).astype(BF16)
    kv = _dot(m, wkv_ref[...])
    k_ref[...] = kv[:, :D_MODEL].astype(BF16)
    v_ref[...] = kv[:, D_MODEL:].astype(BF16)


def _kv_call(mem, mem_norm, w_kv):
    B = mem.shape[0]
    return pl.pallas_call(
        _kv_kernel,
        grid=(B,),
        in_specs=[
            pl.BlockSpec((None, N_MEM, D_MODEL), lambda b: (b, 0, 0)),
            _const_spec((1, D_MODEL)),
            _const_spec((D_MODEL, 2 * D_MODEL)),
        ],
        out_specs=[
            pl.BlockSpec((None, N_MEM, D_MODEL), lambda b: (b, 0, 0)),
            pl.BlockSpec((None, N_MEM, D_MODEL), lambda b: (b, 0, 0)),
        ],
        out_shape=[jax.ShapeDtypeStruct((B, N_MEM, D_MODEL), BF16)] * 2,
        compiler_params=pltpu.CompilerParams(
            dimension_semantics=("arbitrary",), vmem_limit_bytes=V7X_VMEM_LIMIT_BYTES),
        name="kv_proj",
    )(mem, mem_norm, w_kv)


def _row_groups(n_rows):
    return [slice(r, r + GROUP_ROWS) for r in range(0, n_rows, GROUP_ROWS)]


def _trace_round_robin(chains, skew=0):
    waiting = list(chains)
    n_started = 0
    live = []
    turn = 0
    while waiting or live:
        while waiting and turn >= n_started * skew:
            live.append(waiting.pop(0))
            n_started += 1
        for c in list(live):
            try:
                next(c)
            except StopIteration:
                live.remove(c)
        turn += 1


def _load_weights_as_bf16(jobs, sem):
    chunks = []
    for src, dst, stage in jobs:
        rows = src.shape[0] // WEIGHT_CHUNKS
        for c in range(WEIGHT_CHUNKS):
            chunks.append((src.at[pl.ds(c * rows, rows)], dst.at[pl.ds(c * rows, rows)], stage))

    def copy(k):
        src, _, stage = chunks[k]
        return pltpu.make_async_copy(src, stage.at[k % 2], sem.at[k % 2])

    copy(0).start()
    for k, (_, dst, stage) in enumerate(chunks):
        if k + 1 < len(chunks):
            copy(k + 1).start()
        copy(k).wait()
        dst[...] = stage[k % 2].astype(BF16)


def _is_first_step_2d():
    return jnp.logical_and(pl.program_id(0) == 0, pl.program_id(1) == 0)


def _swiglu_residual(x, g, wg_ref, wu_ref, wd_ref):
    u = _rms(x, g).astype(BF16)
    acc = None
    for c0, c1 in FF_CHUNKS:
        gate = _dot(u, wg_ref[:, c0:c1])
        up = _dot(u, wu_ref[:, c0:c1])
        yield
        act = (gate * jax.nn.sigmoid(gate) * up).astype(BF16)
        part = _dot(act, wd_ref[c0:c1, :])
        acc = part if acc is None else acc + part
        yield
    return x + 0.5 * acc


_HBM = pl.BlockSpec(memory_space=pl.ANY)


def _swiglu_weight_scratch(D):
    return [
        pltpu.VMEM((D, D_FF), BF16),
        pltpu.VMEM((D, D_FF), BF16),
        pltpu.VMEM((D_FF, D), BF16),
        pltpu.VMEM((2, D // WEIGHT_CHUNKS, D_FF), F32),
        pltpu.VMEM((2, D_FF // WEIGHT_CHUNKS, D), F32),
        pltpu.SemaphoreType.DMA((2,)),
    ]


def _ffn1_kernel(x_ref, n1_ref, wg_hbm, wu_hbm, wd_hbm, n2_ref, win_hbm, h_ref, p_ref, s_ref,
                 wg_ref, wu_ref, wd_ref, stage_up, stage_down, sem, wps_ref):
    @pl.when(_is_first_step_2d())
    def _():
        n_ps = D_POOL + D_SSM
        _load_weights_as_bf16([
            (wg_hbm, wg_ref, stage_up),
            (wu_hbm, wu_ref, stage_up),
            (wd_hbm, wd_ref, stage_down),
            (win_hbm.at[:, pl.ds(0, n_ps)], wps_ref, stage_up.at[:, :, pl.ds(0, n_ps)]),
        ], sem)

    def chain(rows):
        h = yield from _swiglu_residual(x_ref[rows, :], n1_ref[...], wg_ref, wu_ref, wd_ref)
        h_ref[rows, :] = h
        u = _rms(h, n2_ref[...]).astype(BF16)
        ps = _dot(u, wps_ref[...])
        p_ref[rows, :] = ps[:, :D_POOL]
        s_ref[rows, :] = ps[:, D_POOL:]

    _trace_round_robin([chain(rows) for rows in _row_groups(x_ref.shape[0])], skew=FFN_SKEW)


def _ffn1_call(x, n1, wg, wu, wd, n2, w_in):
    B, L, D = x.shape
    T = FFN_TILE
    tok = pl.BlockSpec((None, T, D), lambda b, i: (b, i, 0))
    return pl.pallas_call(
        _ffn1_kernel,
        grid=(B, L // T),
        in_specs=[tok, _const_spec((1, D)), _HBM, _HBM, _HBM, _const_spec((1, D)), _HBM],
        out_specs=[
            tok,
            pl.BlockSpec((None, T, D_POOL), lambda b, i: (b, i, 0)),
            pl.BlockSpec((None, T, D_SSM), lambda b, i: (b, i, 0)),
        ],
        out_shape=[
            jax.ShapeDtypeStruct((B, L, D), F32),
            jax.ShapeDtypeStruct((B, L, D_POOL), F32),
            jax.ShapeDtypeStruct((B, L, D_SSM), F32),
        ],
        scratch_shapes=_swiglu_weight_scratch(D) + [pltpu.VMEM((D, D_POOL + D_SSM), BF16)],
        compiler_params=pltpu.CompilerParams(
            dimension_semantics=("arbitrary", "arbitrary"), vmem_limit_bytes=V7X_VMEM_LIMIT_BYTES),
        name="ffn1",
    )(x, n1, wg, wu, wd, n2, w_in)


def _ffn2_kernel(h_ref, n_ref, wg_hbm, wu_hbm, wd_hbm, fn_ref, o_ref,
                 wg_ref, wu_ref, wd_ref, stage_up, stage_down, sem):
    @pl.when(_is_first_step_2d())
    def _():
        _load_weights_as_bf16([
            (wg_hbm, wg_ref, stage_up),
            (wu_hbm, wu_ref, stage_up),
            (wd_hbm, wd_ref, stage_down),
        ], sem)

    def chain(rows):
        h = yield from _swiglu_residual(h_ref[rows, :], n_ref[...], wg_ref, wu_ref, wd_ref)
        o_ref[rows, :] = _rms(h, fn_ref[...])

    _trace_round_robin([chain(rows) for rows in _row_groups(h_ref.shape[0])], skew=FFN_SKEW)


def _ffn2_call(h, n, wg, wu, wd, fn):
    B, L, D = h.shape
    T = FFN_TILE
    tok = pl.BlockSpec((None, T, D), lambda b, i: (b, i, 0))
    return pl.pallas_call(
        _ffn2_kernel,
        grid=(B, L // T),
        in_specs=[tok, _const_spec((1, D)), _HBM, _HBM, _HBM, _const_spec((1, D))],
        out_specs=tok,
        out_shape=jax.ShapeDtypeStruct((B, L, D), F32),
        scratch_shapes=_swiglu_weight_scratch(D),
        compiler_params=pltpu.CompilerParams(
            dimension_semantics=("arbitrary", "arbitrary"), vmem_limit_bytes=V7X_VMEM_LIMIT_BYTES),
        name="ffn2",
    )(h, n, wg, wu, wd, fn)


def _ssm_kernel(sf_ref, sb_ref, are_ref, aim_ref, ldt_ref, bre_ref, bim_ref, crt_ref, cit_ref, d_ref,
                yf_ref, yb_ref, abar_scr, bmat_scr, cmat_scr, state_scr, bu_scr, xb_scr):
    i = pl.program_id(0)
    S = D_STATE
    n_b, steps, _ = sf_ref.shape
    R = steps * n_b

    @pl.when(i == 0)
    def _():
        for d in range(2):
            ar = are_ref[d]
            ai = aim_ref[d]
            dt = jnp.exp(ldt_ref[d])
            mag = jnp.exp(dt * ar)
            ang = dt * ai
            abr = mag * jnp.cos(ang)
            abi = mag * jnp.sin(ang)
            den = ar * ar + ai * ai
            nr = abr - 1.0
            qr = (nr * ar + abi * ai) / den
            qi = (abi * ar - nr * ai) / den
            abar_scr[d, 0] = jnp.broadcast_to(abr, (8, S))
            abar_scr[d, 1] = jnp.broadcast_to(abi, (8, S))
            br = bre_ref[d]
            bi = bim_ref[d]
            bmat_scr[d, :, :S] = (qr * br - qi * bi).astype(BF16)
            bmat_scr[d, :, S:] = (qr * bi + qi * br).astype(BF16)
            cmat_scr[d, :S, :] = crt_ref[d]
            cmat_scr[d, S:, :] = -cit_ref[d]
        state_scr[...] = jnp.zeros_like(state_scr)

    sub = SSM_SUBTILE
    n_sub = steps // sub
    halves = [(slice(c0, c0 + SCAN_LANES), slice(S + c0, S + c0 + SCAN_LANES))
              for c0 in range(0, S, SCAN_LANES)]
    def chain(d, s_ref, y_ref):
        order = list(range(n_sub)) if d == 0 else list(range(n_sub - 1, -1, -1))

        def input_map(j):
            s_nat = s_ref[:, j * sub:(j + 1) * sub, :]
            s_tb = jnp.swapaxes(s_nat, 0, 1).reshape(sub * n_b, D_SSM)
            bu_scr[d, j * sub * n_b:(j + 1) * sub * n_b, :] = _dot(s_tb.astype(BF16), bmat_scr[d])

        coef = [(abar_scr[d, 0, :, re], abar_scr[d, 1, :, re]) for re, _ in halves]
        state = [(state_scr[d, 0, :, re], state_scr[d, 1, :, re]) for re, _ in halves]
        input_map(order[0])
        yield
        for pos, j in enumerate(order):
            if pos + 1 < n_sub:
                input_map(order[pos + 1])
            for hf, (re, im) in enumerate(halves):
                ar, ai = coef[hf]
                xr, xi = state[hf]
                for k in range(0, sub, 2):
                    lo = j * sub + (k if d == 0 else sub - 2 - k)
                    got = {}
                    for t in ((lo, lo + 1) if d == 0 else (lo + 1, lo)):
                        rows = slice(n_b * t, n_b * (t + 1))
                        nxr = ar * xr - ai * xi + bu_scr[d, rows, re]
                        nxi = ar * xi + ai * xr + bu_scr[d, rows, im]
                        xr, xi = nxr, nxi
                        got[t] = (nxr, nxi)
                    rows2 = slice(n_b * lo, n_b * (lo + 2))
                    xb_scr[d, rows2, re] = jnp.concatenate([got[lo][0], got[lo + 1][0]], axis=0).astype(BF16)
                    xb_scr[d, rows2, im] = jnp.concatenate([got[lo][1], got[lo + 1][1]], axis=0).astype(BF16)
                state[hf] = (xr, xi)
            rows_j = slice(j * sub * n_b, (j + 1) * sub * n_b)
            y_tb = _dot(xb_scr[d, rows_j, :], cmat_scr[d])
            y = jnp.swapaxes(y_tb.reshape(sub, n_b, D_SSM), 0, 1)
            if d == 0:
                y = y + sf_ref[:, j * sub:(j + 1) * sub, :] * d_ref[...]
            y_ref[:, j * sub:(j + 1) * sub, :] = y
            yield
        for hf, (re, _) in enumerate(halves):
            state_scr[d, 0, :, re] = state[hf][0]
            state_scr[d, 1, :, re] = state[hf][1]

    _trace_round_robin([chain(0, sf_ref, yf_ref), chain(1, sb_ref, yb_ref)])


def _ssm_call(s, a_re, a_im, log_dt, b_re_bd, b_im_bd, c_re_t, c_im_t, d_skip):
    B, L, _ = s.shape
    assert B == V7X_SUBLANES, "the scan keeps one batch element per sublane"
    T = SSM_TILE
    n = L // T
    S = D_STATE
    fwd = pl.BlockSpec((B, T, D_SSM), lambda i: (0, i, 0))
    bwd = pl.BlockSpec((B, T, D_SSM), lambda i: (0, n - 1 - i, 0))
    return pl.pallas_call(
        _ssm_kernel,
        grid=(n,),
        in_specs=[
            fwd, bwd,
            _const_spec((2, 1, S)), _const_spec((2, 1, S)), _const_spec((2, 1, S)),
            _const_spec((2, D_SSM, S)), _const_spec((2, D_SSM, S)),
            _const_spec((2, S, D_SSM)), _const_spec((2, S, D_SSM)),
            _const_spec((1, D_SSM)),
        ],
        out_specs=[fwd, bwd],
        out_shape=[jax.ShapeDtypeStruct((B, L, D_SSM), F32)] * 2,
        scratch_shapes=[
            pltpu.VMEM((2, 2, 8, S), F32),
            pltpu.VMEM((2, D_SSM, 2 * S), BF16),
            pltpu.VMEM((2, 2 * S, D_SSM), BF16),
            pltpu.VMEM((2, 2, 8, S), F32),
            pltpu.VMEM((2, B * T, 2 * S), F32),
            pltpu.VMEM((2, B * T, 2 * S), BF16),
        ],
        compiler_params=pltpu.CompilerParams(
            dimension_semantics=("arbitrary",), vmem_limit_bytes=V7X_VMEM_LIMIT_BYTES),
        name="ssm_scan",
    )(s, s, a_re, a_im, log_dt, b_re_bd, b_im_bd, c_re_t, c_im_t, d_skip)


def _window_sum(v, w):
    n = v.shape[0]
    ahead = lambda a, k: pltpu.roll(a, n - k, axis=0)
    behind = lambda a, k: pltpu.roll(a, k, axis=0)
    if w == 2:
        return v + behind(v, 1)
    cur, span = v + ahead(v, 1), 2
    while span < w // 2:
        cur, span = cur + ahead(cur, span), 2 * span
    return cur + behind(cur, span)


def _mix_kernel(h_ref, p_ref, pl_ref, pr_ref, yf_ref, yb_ref, k_ref, v_ref,
                nmix_ref, wgate_ref, poolw_ref, pscale_ref, wpp_ref, wglu_ref, wmo_ref,
                nx_ref, wq_ref, wxo_ref, o_ref, *, seq_len):
    i = pl.program_id(1)
    n_i = pl.num_programs(1)
    T = h_ref.shape[0]
    H = POOL_HALO

    halo_l = jnp.where(i > 0, pl_ref[...], 0.0)
    halo_r = jnp.where(i < n_i - 1, pr_ref[...], 0.0)
    edge = V7X_SUBLANES
    t_edge = lax.broadcasted_iota(jnp.int32, (edge, POOL_GROUP), 0)
    pooled_groups = []
    for g, w in enumerate(POOL_WINDOWS):
        left = w // 2
        right = w - 1 - left
        lanes = slice(g * POOL_GROUP, (g + 1) * POOL_GROUP)
        v = jnp.concatenate([halo_l[:, lanes], p_ref[:, lanes], halo_r[:, lanes]], axis=0)
        tot = _window_sum(v, w)[H:H + T]
        self_ = v[H:H + T]

        def clipped(t0, rows):
            t = i * T + t0 + t_edge
            cnt = jnp.minimum(t + (right + 1), seq_len) - jnp.maximum(t - left, 0)
            return tot[rows] / cnt.astype(F32) - self_[rows]

        pooled_groups.append(jnp.concatenate([
            clipped(0, slice(0, edge)),
            tot[edge:T - edge] * (1.0 / w) - self_[edge:T - edge],
            clipped(T - edge, slice(T - edge, T)),
        ], axis=0).astype(BF16))

    def chain(rows):
        h = h_ref[rows, :]
        mixed = [_dot(pooled_groups[g][rows], poolw_ref[g]) for g in range(len(POOL_WINDOWS))]
        yield
        mixed = jnp.concatenate(mixed, axis=-1) * pscale_ref[...]
        z_pool = _dot(mixed.astype(BF16), wpp_ref[...])
        yield

        y = jax.nn.gelu(yf_ref[rows, :] + yb_ref[rows, :]).astype(BF16)
        glu = _dot(y, wglu_ref[...])
        yield
        z_ssm = glu[:, :D_MODEL] * jax.nn.sigmoid(glu[:, D_MODEL:])

        u = _rms(h, nmix_ref[...]).astype(BF16)
        gates = _dot(u, wgate_ref[...])
        yield
        merged = (jax.nn.sigmoid(gates[:, :D_MODEL]) * z_pool
                  + jax.nn.sigmoid(gates[:, D_MODEL:]) * z_ssm)
        h = h + _dot(merged.astype(BF16), wmo_ref[...])
        yield

        q = _dot(_rms(h, nx_ref[...]).astype(BF16), wq_ref[...])
        yield
        head_cols = [slice(hd * XHEAD_DIM, (hd + 1) * XHEAD_DIM) for hd in range(N_XHEADS)]
        scores = [lax.dot_general(q[:, cols].astype(BF16), k_ref[:, cols], (((1,), (1,)), ((), ())),
                                  preferred_element_type=F32) / math.sqrt(XHEAD_DIM) for cols in head_cols]
        yield
        heads = []
        for sc, cols in zip(scores, head_cols):
            e = jnp.exp(sc - jnp.max(sc, axis=-1, keepdims=True))
            probs = e / jnp.sum(e, axis=-1, keepdims=True)
            heads.append(_dot(probs.astype(BF16), v_ref[:, cols]))
        yield
        o = jnp.concatenate(heads, axis=-1).astype(BF16)
        o_ref[rows, :] = h + _dot(o, wxo_ref[...])

    _trace_round_robin([chain(rows) for rows in _row_groups(T)], skew=MIX_SKEW)


def _mix_call(h, p, yf, yb, k, v, nmix, wgate, poolw, pscale, wpp, wglu, wmo, nx, wq, wxo):
    B, L, D = h.shape
    T = MIX_TILE
    H = POOL_HALO
    per_tile = T // H
    n_halo = L // H
    tok = pl.BlockSpec((None, T, D), lambda b, i: (b, i, 0))
    y_spec = pl.BlockSpec((None, T, D_SSM), lambda b, i: (b, i, 0))
    mem_spec = pl.BlockSpec((None, N_MEM, D), lambda b, i: (b, 0, 0))
    return pl.pallas_call(
        functools.partial(_mix_kernel, seq_len=L),
        grid=(B, L // T),
        in_specs=[
            tok,
            pl.BlockSpec((None, T, D_POOL), lambda b, i: (b, i, 0)),
            pl.BlockSpec((None, H, D_POOL), lambda b, i: (b, jnp.maximum(i * per_tile - 1, 0), 0)),
            pl.BlockSpec((None, H, D_POOL), lambda b, i: (b, jnp.minimum((i + 1) * per_tile, n_halo - 1), 0)),
            y_spec, y_spec,
            mem_spec, mem_spec,
            _const_spec((1, D)),
            _const_spec((D, 2 * D)),
            _const_spec((len(POOL_WINDOWS), POOL_GROUP, POOL_GROUP)),
            _const_spec((1, D_POOL)),
            _const_spec((D_POOL, D)),
            _const_spec((D_SSM, 2 * D)),
            _const_spec((D, D)),
            _const_spec((1, D)),
            _const_spec((D, D)),
            _const_spec((D, D)),
        ],
        out_specs=tok,
        out_shape=jax.ShapeDtypeStruct((B, L, D), F32),
        compiler_params=pltpu.CompilerParams(
            dimension_semantics=("arbitrary", "arbitrary"), vmem_limit_bytes=V7X_VMEM_LIMIT_BYTES),
        name="mix_attn",
    )(h, p, p, p, yf, yb, k, v, nmix, wgate, poolw, pscale, wpp, wglu, wmo, nx, wq, wxo)


def _block_diag_in(w):
    G, P, Hh = w.shape
    eye = jnp.eye(G, dtype=w.dtype)
    return jnp.einsum("gph,gk->ghkp", w, eye).reshape(G * Hh, G * P)


def _block_diag_out(w):
    G, Hh, P = w.shape
    eye = jnp.eye(G, dtype=w.dtype)
    return jnp.einsum("ghp,gk->gpkh", w, eye).reshape(G * P, G * Hh)


def kernel(x, mem, ffn1_norm, ffn1_w_gate, ffn1_w_up, ffn1_w_down, mix_norm, w_in, pool_w, pool_scale,
           w_pool_proj, ssm_a_re, ssm_a_im, ssm_log_dt, ssm_b_re, ssm_b_im, ssm_c_re, ssm_c_im, ssm_d,
           w_glu_val, w_glu_gate, w_mix_out, xattn_norm, mem_norm, w_q, w_kv, w_xo, ffn2_norm,
           ffn2_w_gate, ffn2_w_up, ffn2_w_down, final_norm):
    B, L, D = x.shape
    depth = ffn1_norm.shape[0]
    row = lambda v: v.reshape(1, -1)
    bf = lambda w: w.astype(BF16)
    h = x
    for l in range(depth):
        k_mem, v_mem = _kv_call(mem, row(mem_norm[l]), bf(w_kv[l]))

        h, p, s = _ffn1_call(
            h, row(ffn1_norm[l]), ffn1_w_gate[l], ffn1_w_up[l], ffn1_w_down[l], row(mix_norm[l]), w_in[l])

        G, P = N_SSM_GROUPS, SSM_STATE
        per_state = lambda a: a.reshape(2, 1, G * P)
        log_dt = jnp.broadcast_to(ssm_log_dt[l][:, :, None], (2, G, P))
        yf, yb = _ssm_call(
            s, per_state(ssm_a_re[l]), per_state(ssm_a_im[l]), per_state(log_dt),
            jax.vmap(_block_diag_in)(ssm_b_re[l]), jax.vmap(_block_diag_in)(ssm_b_im[l]),
            bf(jax.vmap(_block_diag_out)(ssm_c_re[l])), bf(jax.vmap(_block_diag_out)(ssm_c_im[l])),
            row(ssm_d[l]))

        h = _mix_call(
            h, p, yf, yb, k_mem, v_mem,
            row(mix_norm[l]), bf(w_in[l][:, D_POOL + D_SSM:]), bf(pool_w[l]), row(pool_scale[l]),
            bf(w_pool_proj[l]), bf(jnp.concatenate([w_glu_val[l], w_glu_gate[l]], axis=1)),
            bf(w_mix_out[l]), row(xattn_norm[l]), bf(w_q[l]), bf(w_xo[l]))

        last = l == depth - 1
        assert last, "deeper stacks need a non-final variant of the second SwiGLU kernel"
        h = _ffn2_call(h, row(ffn2_norm[l]), ffn2_w_gate[l], ffn2_w_up[l], ffn2_w_down[l], row(final_norm))
    return h
```

```python
import functools
import math

import jax
import jax.numpy as jnp
from jax import lax
from jax.experimental import pallas as pl
from jax.experimental.pallas import tpu as pltpu

F32 = jnp.float32
BF16 = jnp.bfloat16

D_MODEL = 1024
N_MEM = 256
D_FF = 2816
D_POOL = 512
POOL_WINDOWS = (2, 4, 8, 16)
POOL_GROUP = 128
D_SSM = 256
SSM_GROUP = 16
N_SSM_GROUPS = 16
SSM_STATE = 64
D_STATE = N_SSM_GROUPS * SSM_STATE
N_XHEADS = 4
XHEAD_DIM = 256
EPS = 1e-6

V7X_VMEM_LIMIT_BYTES = 58 * 1024 * 1024
V7X_SUBLANES = 8
V7X_BF16_SUBLANES = 16
POOL_HALO = 16

FFN_TILE = 1024
MIX_TILE = 1024
SSM_TILE = 128
FFN_SKEW = 1
MIX_SKEW = 1
WEIGHT_CHUNKS = 32
GROUP_ROWS = 256
FF_CHUNKS = ((0, 768), (768, 1536), (1536, 2304), (2304, 2816))
SSM_SUBTILE = 32
SCAN_LANES = 512


def _rms(v, g):
    r = lax.rsqrt(jnp.mean(v * v, axis=-1, keepdims=True) + EPS)
    return (v * r) * g


def _dot(a, b):
    return jnp.dot(a, b, preferred_element_type=F32)


def _const_spec(shape):
    nd = len(shape)
    return pl.BlockSpec(shape, lambda *_: (0,) * nd)


def _kv_kernel(mem_ref, g_ref, wkv_ref, k_ref, v_ref):
    m = _rms(mem_ref[...], g_ref[...]).astype(BF16)
    kv = _dot(m, wkv_ref[...])
    k_ref[...] = kv[:, :D_MODEL].astype(BF16)
    v_ref[...] = kv[:, D_MODEL:].astype(BF16)


def _kv_call(mem, mem_norm, w_kv):
    B = mem.shape[0]
    return pl.pallas_call(
        _kv_kernel,
        grid=(B,),
        in_specs=[
            pl.BlockSpec((None, N_MEM, D_MODEL), lambda b: (b, 0, 0)),
            _const_spec((1, D_MODEL)),
            _const_spec((D_MODEL, 2 * D_MODEL)),
        ],
        out_specs=[
            pl.BlockSpec((None, N_MEM, D_MODEL), lambda b: (b, 0, 0)),
            pl.BlockSpec((None, N_MEM, D_MODEL), lambda b: (b, 0, 0)),
        ],
        out_shape=[jax.ShapeDtypeStruct((B, N_MEM, D_MODEL), BF16)] * 2,
        compiler_params=pltpu.CompilerParams(
            dimension_semantics=("arbitrary",), vmem_limit_bytes=V7X_VMEM_LIMIT_BYTES),
        name="kv_proj",
    )(mem, mem_norm, w_kv)


def _row_groups(n_rows):
    return [slice(r, r + GROUP_ROWS) for r in range(0, n_rows, GROUP_ROWS)]


def _trace_round_robin(chains, skew=0):
    waiting = list(chains)
    n_started = 0
    live = []
    turn = 0
    while waiting or live:
        while waiting and turn >= n_started * skew:
            live.append(waiting.pop(0))
            n_started += 1
        for c in list(live):
            try:
                next(c)
            except StopIteration:
                live.remove(c)
        turn += 1


def _load_weights_as_bf16(jobs, sem):
    chunks = []
    for src, dst, stage in jobs:
        rows = stage.shape[1]
        for r0 in range(0, src.shape[0], rows):
            chunks.append((src.at[pl.ds(r0, rows)], dst.at[pl.ds(r0, rows)], stage))

    def copy(k):
        src, _, stage = chunks[k]
        return pltpu.make_async_copy(src, stage.at[k % 2], sem.at[k % 2])

    copy(0).start()
    for k, (_, dst, stage) in enumerate(chunks):
        if k + 1 < len(chunks):
            copy(k + 1).start()
        copy(k).wait()
        dst[...] = stage[k % 2].astype(BF16)


def _is_first_step_2d():
    return jnp.logical_and(pl.program_id(0) == 0, pl.program_id(1) == 0)


def _swiglu_residual(x, g, wg_ref, wu_ref, wd_ref):
    u = _rms(x, g).astype(BF16)
    acc = None
    for c0, c1 in FF_CHUNKS:
        gate = _dot(u, wg_ref[:, c0:c1])
        up = _dot(u, wu_ref[:, c0:c1])
        yield
        act = (gate * jax.nn.sigmoid(gate) * up).astype(BF16)
        part = _dot(act, wd_ref[c0:c1, :])
        acc = part if acc is None else acc + part
        yield
    return x + 0.5 * acc


_HBM = pl.BlockSpec(memory_space=pl.ANY)


def _swiglu_weight_scratch(D):
    return [
        pltpu.VMEM((D, D_FF), BF16),
        pltpu.VMEM((D, D_FF), BF16),
        pltpu.VMEM((D_FF, D), BF16),
        pltpu.VMEM((2, _cast_block_rows(D, WEIGHT_CHUNKS), D_FF), F32),
        pltpu.VMEM((2, _cast_block_rows(D_FF, WEIGHT_CHUNKS), D), F32),
        pltpu.SemaphoreType.DMA((2,)),
    ]


def _cast_block_rows(n_rows, n_steps):
    rows = V7X_BF16_SUBLANES
    while n_rows % rows or n_rows // rows > n_steps:
        rows += V7X_BF16_SUBLANES
    return rows


def _cast_specs(to_cast, n_steps, step_of):
    specs = []
    for w in to_cast:
        rows = _cast_block_rows(w.shape[0], n_steps)
        last = w.shape[0] // rows - 1
        specs.append(pl.BlockSpec(
            (rows, w.shape[1]), lambda *idx, last=last: (jnp.minimum(step_of(*idx), last), 0)))
    return specs


def _cast_blocks(srcs, dsts):
    for src, dst in zip(srcs, dsts):
        dst[...] = src[...].astype(BF16)


def _ffn1_kernel(*refs, n_cast):
    (x_ref, n1_ref, wg_hbm, wu_hbm, wd_hbm, n2_ref, win_hbm), refs = refs[:7], refs[7:]
    cast_src, refs = refs[:n_cast], refs[n_cast:]
    (h_ref, p_ref, s_ref), refs = refs[:3], refs[3:]
    cast_dst, refs = refs[:n_cast], refs[n_cast:]
    wg_ref, wu_ref, wd_ref, stage_up, stage_down, sem, wps_ref = refs

    @pl.when(_is_first_step_2d())
    def _():
        n_ps = D_POOL + D_SSM
        _load_weights_as_bf16([
            (wg_hbm, wg_ref, stage_up),
            (wu_hbm, wu_ref, stage_up),
            (wd_hbm, wd_ref, stage_down),
            (win_hbm.at[:, pl.ds(0, n_ps)], wps_ref, stage_up.at[:, :, pl.ds(0, n_ps)]),
        ], sem)

    def chain(rows):
        h = yield from _swiglu_residual(x_ref[rows, :], n1_ref[...], wg_ref, wu_ref, wd_ref)
        h_ref[rows, :] = h
        u = _rms(h, n2_ref[...]).astype(BF16)
        ps = _dot(u, wps_ref[...])
        p_ref[rows, :] = ps[:, :D_POOL]
        s_ref[rows, :] = ps[:, D_POOL:]

    _trace_round_robin([chain(rows) for rows in _row_groups(x_ref.shape[0])], skew=FFN_SKEW)

    _cast_blocks(cast_src, cast_dst)


def _ffn1_call(x, n1, wg, wu, wd, n2, w_in, to_cast):
    B, L, D = x.shape
    T = FFN_TILE
    n_i = L // T
    n_steps = B * n_i
    tok = pl.BlockSpec((None, T, D), lambda b, i: (b, i, 0))
    cast_specs = _cast_specs(to_cast, n_steps, lambda b, i: b * n_i + i)
    outs = pl.pallas_call(
        functools.partial(_ffn1_kernel, n_cast=len(to_cast)),
        grid=(B, n_i),
        in_specs=[tok, _const_spec((1, D)), _HBM, _HBM, _HBM, _const_spec((1, D)), _HBM] + cast_specs,
        out_specs=[
            tok,
            pl.BlockSpec((None, T, D_POOL), lambda b, i: (b, i, 0)),
            pl.BlockSpec((None, T, D_SSM), lambda b, i: (b, i, 0)),
        ] + cast_specs,
        out_shape=[
            jax.ShapeDtypeStruct((B, L, D), F32),
            jax.ShapeDtypeStruct((B, L, D_POOL), F32),
            jax.ShapeDtypeStruct((B, L, D_SSM), F32),
        ] + [jax.ShapeDtypeStruct(w.shape, BF16) for w in to_cast],
        scratch_shapes=_swiglu_weight_scratch(D) + [pltpu.VMEM((D, D_POOL + D_SSM), BF16)],
        compiler_params=pltpu.CompilerParams(
            dimension_semantics=("arbitrary", "arbitrary"), vmem_limit_bytes=V7X_VMEM_LIMIT_BYTES),
        name="ffn1",
    )(x, n1, wg, wu, wd, n2, w_in, *to_cast)
    return outs[0], outs[1], outs[2], outs[3:]


def _ffn2_kernel(h_ref, n_ref, wg_ref, wu_ref, wd_ref, fn_ref, o_ref):
    def chain(rows):
        h = yield from _swiglu_residual(h_ref[rows, :], n_ref[...], wg_ref, wu_ref, wd_ref)
        o_ref[rows, :] = _rms(h, fn_ref[...])

    _trace_round_robin([chain(rows) for rows in _row_groups(h_ref.shape[0])], skew=FFN_SKEW)


def _ffn2_call(h, n, wg, wu, wd, fn):
    B, L, D = h.shape
    T = FFN_TILE
    tok = pl.BlockSpec((None, T, D), lambda b, i: (b, i, 0))
    return pl.pallas_call(
        _ffn2_kernel,
        grid=(B, L // T),
        in_specs=[tok, _const_spec((1, D)), _const_spec((D, D_FF)), _const_spec((D, D_FF)),
                  _const_spec((D_FF, D)), _const_spec((1, D))],
        out_specs=tok,
        out_shape=jax.ShapeDtypeStruct((B, L, D), F32),
        compiler_params=pltpu.CompilerParams(
            dimension_semantics=("arbitrary", "arbitrary"), vmem_limit_bytes=V7X_VMEM_LIMIT_BYTES),
        name="ffn2",
    )(h, n, wg, wu, wd, fn)


def _ssm_kernel(*refs, n_cast):
    (sf_ref, sb_ref, are_ref, aim_ref, ldt_ref, bre_ref, bim_ref, crt_ref, cit_ref, d_ref), refs = refs[:10], refs[10:]
    cast_src, refs = refs[:n_cast], refs[n_cast:]
    (yf_ref, yb_ref), refs = refs[:2], refs[2:]
    cast_dst, refs = refs[:n_cast], refs[n_cast:]
    abar_scr, bmat_scr, cmat_scr, state_scr, bu_scr, xb_scr = refs
    i = pl.program_id(0)
    S = D_STATE
    n_b, steps, _ = sf_ref.shape
    R = steps * n_b

    @pl.when(i == 0)
    def _():
        for d in range(2):
            ar = are_ref[d]
            ai = aim_ref[d]
            dt = jnp.exp(ldt_ref[d])
            mag = jnp.exp(dt * ar)
            ang = dt * ai
            abr = mag * jnp.cos(ang)
            abi = mag * jnp.sin(ang)
            den = ar * ar + ai * ai
            nr = abr - 1.0
            qr = (nr * ar + abi * ai) / den
            qi = (abi * ar - nr * ai) / den
            abar_scr[d, 0] = jnp.broadcast_to(abr, (8, S))
            abar_scr[d, 1] = jnp.broadcast_to(abi, (8, S))
            br = bre_ref[d]
            bi = bim_ref[d]
            bmat_scr[d, :, :S] = (qr * br - qi * bi).astype(BF16)
            bmat_scr[d, :, S:] = (qr * bi + qi * br).astype(BF16)
            cmat_scr[d, :S, :] = crt_ref[d]
            cmat_scr[d, S:, :] = -cit_ref[d]
        state_scr[...] = jnp.zeros_like(state_scr)

    sub = SSM_SUBTILE
    n_sub = steps // sub
    halves = [(slice(c0, c0 + SCAN_LANES), slice(S + c0, S + c0 + SCAN_LANES))
              for c0 in range(0, S, SCAN_LANES)]
    def chain(d, s_ref, y_ref):
        order = list(range(n_sub)) if d == 0 else list(range(n_sub - 1, -1, -1))

        def input_map(j):
            s_nat = s_ref[:, j * sub:(j + 1) * sub, :]
            s_tb = jnp.swapaxes(s_nat, 0, 1).reshape(sub * n_b, D_SSM)
            bu_scr[d, j * sub * n_b:(j + 1) * sub * n_b, :] = _dot(s_tb.astype(BF16), bmat_scr[d])

        coef = [(abar_scr[d, 0, :, re], abar_scr[d, 1, :, re]) for re, _ in halves]
        state = [(state_scr[d, 0, :, re], state_scr[d, 1, :, re]) for re, _ in halves]
        input_map(order[0])
        yield
        for pos, j in enumerate(order):
            if pos + 1 < n_sub:
                input_map(order[pos + 1])
            for hf, (re, im) in enumerate(halves):
                ar, ai = coef[hf]
                xr, xi = state[hf]
                for k in range(0, sub, 2):
                    lo = j * sub + (k if d == 0 else sub - 2 - k)
                    got = {}
                    for t in ((lo, lo + 1) if d == 0 else (lo + 1, lo)):
                        rows = slice(n_b * t, n_b * (t + 1))
                        nxr = ar * xr - ai * xi + bu_scr[d, rows, re]
                        nxi = ar * xi + ai * xr + bu_scr[d, rows, im]
                        xr, xi = nxr, nxi
                        got[t] = (nxr, nxi)
                    rows2 = slice(n_b * lo, n_b * (lo + 2))
                    xb_scr[d, rows2, re] = jnp.concatenate([got[lo][0], got[lo + 1][0]], axis=0).astype(BF16)
                    xb_scr[d, rows2, im] = jnp.concatenate([got[lo][1], got[lo + 1][1]], axis=0).astype(BF16)
                state[hf] = (xr, xi)
            rows_j = slice(j * sub * n_b, (j + 1) * sub * n_b)
            y_tb = _dot(xb_scr[d, rows_j, :], cmat_scr[d])
            y = jnp.swapaxes(y_tb.reshape(sub, n_b, D_SSM), 0, 1)
            if d == 0:
                y = y + sf_ref[:, j * sub:(j + 1) * sub, :] * d_ref[...]
            y_ref[:, j * sub:(j + 1) * sub, :] = y
            yield
        for hf, (re, _) in enumerate(halves):
            state_scr[d, 0, :, re] = state[hf][0]
            state_scr[d, 1, :, re] = state[hf][1]

    _trace_round_robin([chain(0, sf_ref, yf_ref), chain(1, sb_ref, yb_ref)])
    _cast_blocks(cast_src, cast_dst)


def _ssm_call(s, a_re, a_im, log_dt, b_re_bd, b_im_bd, c_re_t, c_im_t, d_skip, to_cast):
    B, L, _ = s.shape
    assert B == V7X_SUBLANES, "the scan keeps one batch element per sublane"
    T = SSM_TILE
    n = L // T
    S = D_STATE
    fwd = pl.BlockSpec((B, T, D_SSM), lambda i: (0, i, 0))
    bwd = pl.BlockSpec((B, T, D_SSM), lambda i: (0, n - 1 - i, 0))
    cast_specs = _cast_specs(to_cast, n, lambda i: i)
    outs = pl.pallas_call(
        functools.partial(_ssm_kernel, n_cast=len(to_cast)),
        grid=(n,),
        in_specs=[
            fwd, bwd,
            _const_spec((2, 1, S)), _const_spec((2, 1, S)), _const_spec((2, 1, S)),
            _const_spec((2, D_SSM, S)), _const_spec((2, D_SSM, S)),
            _const_spec((2, S, D_SSM)), _const_spec((2, S, D_SSM)),
            _const_spec((1, D_SSM)),
        ] + cast_specs,
        out_specs=[fwd, bwd] + cast_specs,
        out_shape=[jax.ShapeDtypeStruct((B, L, D_SSM), F32)] * 2
                  + [jax.ShapeDtypeStruct(w.shape, BF16) for w in to_cast],
        scratch_shapes=[
            pltpu.VMEM((2, 2, 8, S), F32),
            pltpu.VMEM((2, D_SSM, 2 * S), BF16),
            pltpu.VMEM((2, 2 * S, D_SSM), BF16),
            pltpu.VMEM((2, 2, 8, S), F32),
            pltpu.VMEM((2, B * T, 2 * S), F32),
            pltpu.VMEM((2, B * T, 2 * S), BF16),
        ],
        compiler_params=pltpu.CompilerParams(
            dimension_semantics=("arbitrary",), vmem_limit_bytes=V7X_VMEM_LIMIT_BYTES),
        name="ssm_scan",
    )(s, s, a_re, a_im, log_dt, b_re_bd, b_im_bd, c_re_t, c_im_t, d_skip, *to_cast)
    return outs[0], outs[1], outs[2:]


def _window_sum(v, w):
    n = v.shape[0]
    ahead = lambda a, k: pltpu.roll(a, n - k, axis=0)
    behind = lambda a, k: pltpu.roll(a, k, axis=0)
    if w == 2:
        return v + behind(v, 1)
    cur, span = v + ahead(v, 1), 2
    while span < w // 2:
        cur, span = cur + ahead(cur, span), 2 * span
    return cur + behind(cur, span)


def _mix_kernel(h_ref, p_ref, pl_ref, pr_ref, yf_ref, yb_ref, k_ref, v_ref,
                nmix_ref, wgate_ref, poolw_ref, pscale_ref, wpp_ref, wgluv_ref, wglug_ref, wmo_ref,
                nx_ref, wq_ref, wxo_ref, o_ref, *, seq_len):
    i = pl.program_id(1)
    n_i = pl.num_programs(1)
    T = h_ref.shape[0]
    H = POOL_HALO

    halo_l = jnp.where(i > 0, pl_ref[...], 0.0)
    halo_r = jnp.where(i < n_i - 1, pr_ref[...], 0.0)
    edge = V7X_SUBLANES
    t_edge = lax.broadcasted_iota(jnp.int32, (edge, POOL_GROUP), 0)
    pooled_groups = []
    for g, w in enumerate(POOL_WINDOWS):
        left = w // 2
        right = w - 1 - left
        lanes = slice(g * POOL_GROUP, (g + 1) * POOL_GROUP)
        v = jnp.concatenate([halo_l[:, lanes], p_ref[:, lanes], halo_r[:, lanes]], axis=0)
        tot = _window_sum(v, w)[H:H + T]
        self_ = v[H:H + T]

        def clipped(t0, rows):
            t = i * T + t0 + t_edge
            cnt = jnp.minimum(t + (right + 1), seq_len) - jnp.maximum(t - left, 0)
            return tot[rows] / cnt.astype(F32) - self_[rows]

        pooled_groups.append(jnp.concatenate([
            clipped(0, slice(0, edge)),
            tot[edge:T - edge] * (1.0 / w) - self_[edge:T - edge],
            clipped(T - edge, slice(T - edge, T)),
        ], axis=0).astype(BF16))

    def chain(rows):
        h = h_ref[rows, :]
        mixed = [_dot(pooled_groups[g][rows], poolw_ref[g]) for g in range(len(POOL_WINDOWS))]
        yield
        mixed = jnp.concatenate(mixed, axis=-1) * pscale_ref[...]
        z_pool = _dot(mixed.astype(BF16), wpp_ref[...])
        yield

        y = jax.nn.gelu(yf_ref[rows, :] + yb_ref[rows, :]).astype(BF16)
        glu_val = _dot(y, wgluv_ref[...])
        glu_gate = _dot(y, wglug_ref[...])
        yield
        z_ssm = glu_val * jax.nn.sigmoid(glu_gate)

        u = _rms(h, nmix_ref[...]).astype(BF16)
        gates = _dot(u, wgate_ref[...])
        yield
        merged = (jax.nn.sigmoid(gates[:, :D_MODEL]) * z_pool
                  + jax.nn.sigmoid(gates[:, D_MODEL:]) * z_ssm)
        h = h + _dot(merged.astype(BF16), wmo_ref[...])
        yield

        q = _dot(_rms(h, nx_ref[...]).astype(BF16), wq_ref[...])
        yield
        head_cols = [slice(hd * XHEAD_DIM, (hd + 1) * XHEAD_DIM) for hd in range(N_XHEADS)]
        scores = [lax.dot_general(q[:, cols].astype(BF16), k_ref[:, cols], (((1,), (1,)), ((), ())),
                                  preferred_element_type=F32) / math.sqrt(XHEAD_DIM) for cols in head_cols]
        yield
        heads = []
        for sc, cols in zip(scores, head_cols):
            e = jnp.exp(sc - jnp.max(sc, axis=-1, keepdims=True))
            probs = e / jnp.sum(e, axis=-1, keepdims=True)
            heads.append(_dot(probs.astype(BF16), v_ref[:, cols]))
        yield
        o = jnp.concatenate(heads, axis=-1).astype(BF16)
        o_ref[rows, :] = h + _dot(o, wxo_ref[...])

    _trace_round_robin([chain(rows) for rows in _row_groups(T)], skew=MIX_SKEW)


def _mix_call(h, p, yf, yb, k, v, nmix, wgate, poolw, pscale, wpp, wgluv, wglug, wmo, nx, wq, wxo):
    B, L, D = h.shape
    T = MIX_TILE
    H = POOL_HALO
    per_tile = T // H
    n_halo = L // H
    tok = pl.BlockSpec((None, T, D), lambda b, i: (b, i, 0))
    y_spec = pl.BlockSpec((None, T, D_SSM), lambda b, i: (b, i, 0))
    mem_spec = pl.BlockSpec((None, N_MEM, D), lambda b, i: (b, 0, 0))
    return pl.pallas_call(
        functools.partial(_mix_kernel, seq_len=L),
        grid=(B, L // T),
        in_specs=[
            tok,
            pl.BlockSpec((None, T, D_POOL), lambda b, i: (b, i, 0)),
            pl.BlockSpec((None, H, D_POOL), lambda b, i: (b, jnp.maximum(i * per_tile - 1, 0), 0)),
            pl.BlockSpec((None, H, D_POOL), lambda b, i: (b, jnp.minimum((i + 1) * per_tile, n_halo - 1), 0)),
            y_spec, y_spec,
            mem_spec, mem_spec,
            _const_spec((1, D)),
            _const_spec((D, 2 * D)),
            _const_spec((len(POOL_WINDOWS), POOL_GROUP, POOL_GROUP)),
            _const_spec((1, D_POOL)),
            _const_spec((D_POOL, D)),
            _const_spec((D_SSM, D)),
            _const_spec((D_SSM, D)),
            _const_spec((D, D)),
            _const_spec((1, D)),
            _const_spec((D, D)),
            _const_spec((D, D)),
        ],
        out_specs=tok,
        out_shape=jax.ShapeDtypeStruct((B, L, D), F32),
        compiler_params=pltpu.CompilerParams(
            dimension_semantics=("arbitrary", "arbitrary"), vmem_limit_bytes=V7X_VMEM_LIMIT_BYTES),
        name="mix_attn",
    )(h, p, p, p, yf, yb, k, v, nmix, wgate, poolw, pscale, wpp, wgluv, wglug, wmo, nx, wq, wxo)


def _block_diag_in(w):
    G, P, Hh = w.shape
    eye = jnp.eye(G, dtype=w.dtype)
    return jnp.einsum("gph,gk->ghkp", w, eye).reshape(G * Hh, G * P)


def _block_diag_out(w):
    G, Hh, P = w.shape
    eye = jnp.eye(G, dtype=w.dtype)
    return jnp.einsum("ghp,gk->gpkh", w, eye).reshape(G * P, G * Hh)


def kernel(x, mem, ffn1_norm, ffn1_w_gate, ffn1_w_up, ffn1_w_down, mix_norm, w_in, pool_w, pool_scale,
           w_pool_proj, ssm_a_re, ssm_a_im, ssm_log_dt, ssm_b_re, ssm_b_im, ssm_c_re, ssm_c_im, ssm_d,
           w_glu_val, w_glu_gate, w_mix_out, xattn_norm, mem_norm, w_q, w_kv, w_xo, ffn2_norm,
           ffn2_w_gate, ffn2_w_up, ffn2_w_down, final_norm):
    B, L, D = x.shape
    depth = ffn1_norm.shape[0]
    row = lambda v: v.reshape(1, -1)
    bf = lambda w: w.astype(BF16)
    h = x
    for l in range(depth):
        h, p, s, cast = _ffn1_call(
            h, row(ffn1_norm[l]), ffn1_w_gate[l], ffn1_w_up[l], ffn1_w_down[l], row(mix_norm[l]), w_in[l],
            [w_kv[l], w_in[l], pool_w[l].reshape(-1, POOL_GROUP), w_pool_proj[l], w_glu_val[l],
             w_glu_gate[l], w_mix_out[l], w_q[l], w_xo[l]])
        w_kv_b, w_in_b, pool_w_b, w_pp_b, w_gluv_b, w_glug_b, w_mo_b, w_q_b, w_xo_b = cast

        k_mem, v_mem = _kv_call(mem, row(mem_norm[l]), w_kv_b)

        G, P = N_SSM_GROUPS, SSM_STATE
        per_state = lambda a: a.reshape(2, 1, G * P)
        log_dt = jnp.broadcast_to(ssm_log_dt[l][:, :, None], (2, G, P))
        yf, yb, (ffn2_wg_b, ffn2_wu_b, ffn2_wd_b) = _ssm_call(
            s, per_state(ssm_a_re[l]), per_state(ssm_a_im[l]), per_state(log_dt),
            jax.vmap(_block_diag_in)(ssm_b_re[l]), jax.vmap(_block_diag_in)(ssm_b_im[l]),
            bf(jax.vmap(_block_diag_out)(ssm_c_re[l])), bf(jax.vmap(_block_diag_out)(ssm_c_im[l])),
            row(ssm_d[l]), [ffn2_w_gate[l], ffn2_w_up[l], ffn2_w_down[l]])

        h = _mix_call(
            h, p, yf, yb, k_mem, v_mem,
            row(mix_norm[l]), w_in_b[:, D_POOL + D_SSM:], pool_w_b.reshape(pool_w[l].shape), row(pool_scale[l]),
            w_pp_b, w_gluv_b, w_glug_b, w_mo_b, row(xattn_norm[l]), w_q_b, w_xo_b)

        last = l == depth - 1
        assert last, "deeper stacks need a non-final variant of the second SwiGLU kernel"
        h = _ffn2_call(h, row(ffn2_norm[l]), ffn2_wg_b, ffn2_wu_b, ffn2_wd_b, row(final_norm))
    return h
```

```python
import functools
import math

import jax
import jax.numpy as jnp
from jax import lax
from jax.experimental import pallas as pl
from jax.experimental.pallas import tpu as pltpu

F32 = jnp.float32
BF16 = jnp.bfloat16

D_MODEL = 1024
N_MEM = 256
D_FF = 2816
D_POOL = 512
POOL_WINDOWS = (2, 4, 8, 16)
POOL_GROUP = 128
D_SSM = 256
SSM_GROUP = 16
N_SSM_GROUPS = 16
SSM_STATE = 64
D_STATE = N_SSM_GROUPS * SSM_STATE
N_XHEADS = 4
XHEAD_DIM = 256
EPS = 1e-6

V7X_VMEM_LIMIT_BYTES = 58 * 1024 * 1024
V7X_SUBLANES = 8
V7X_BF16_SUBLANES = 16
POOL_HALO = 16

FFN_TILE = 1024
MIX_TILE = 1024
SSM_TILE = 128
FFN_SKEW = 1
MIX_SKEW = 1
WEIGHT_CHUNKS = 32
GROUP_ROWS = 256
FF_CHUNKS = ((0, 768), (768, 1536), (1536, 2304), (2304, 2816))
SSM_SUBTILE = 32
SCAN_LANES = 512


def _rms(v, g):
    r = lax.rsqrt(jnp.mean(v * v, axis=-1, keepdims=True) + EPS)
    return (v * r) * g


def _dot(a, b):
    return jnp.dot(a, b, preferred_element_type=F32)


def _const_spec(shape):
    nd = len(shape)
    return pl.BlockSpec(shape, lambda *_: (0,) * nd)


def _kv_kernel(*refs, n_cast):
    (mem_ref, g_ref, wkv_ref), refs = refs[:3], refs[3:]
    cast_src, refs = refs[:n_cast], refs[n_cast:]
    (k_ref, v_ref), refs = refs[:2], refs[2:]
    cast_dst, (wkv_bf16,) = refs[:n_cast], refs[n_cast:]

    @pl.when(pl.program_id(0) == 0)
    def _():
        wkv_bf16[...] = wkv_ref[...].astype(BF16)

    m = _rms(mem_ref[...], g_ref[...]).astype(BF16)
    kv = _dot(m, wkv_bf16[...])
    k_ref[...] = kv[:, :D_MODEL].astype(BF16)
    v_ref[...] = kv[:, D_MODEL:].astype(BF16)
    _cast_blocks(cast_src, cast_dst)


def _kv_call(mem, mem_norm, w_kv, to_cast):
    B = mem.shape[0]
    per_batch = pl.BlockSpec((None, N_MEM, D_MODEL), lambda b: (b, 0, 0))
    cast_specs = _cast_specs(to_cast, B, lambda b: b)
    outs = pl.pallas_call(
        functools.partial(_kv_kernel, n_cast=len(to_cast)),
        grid=(B,),
        in_specs=[per_batch, _const_spec((1, D_MODEL)), _const_spec((D_MODEL, 2 * D_MODEL))] + cast_specs,
        out_specs=[per_batch, per_batch] + cast_specs,
        out_shape=[jax.ShapeDtypeStruct((B, N_MEM, D_MODEL), BF16)] * 2
                  + [jax.ShapeDtypeStruct(w.shape, BF16) for w in to_cast],
        scratch_shapes=[pltpu.VMEM((D_MODEL, 2 * D_MODEL), BF16)],
        compiler_params=pltpu.CompilerParams(
            dimension_semantics=("arbitrary",), vmem_limit_bytes=V7X_VMEM_LIMIT_BYTES),
        name="kv_proj",
    )(mem, mem_norm, w_kv, *to_cast)
    return outs[0], outs[1], outs[2:]


def _row_groups(n_rows):
    return [slice(r, r + GROUP_ROWS) for r in range(0, n_rows, GROUP_ROWS)]


def _trace_round_robin(chains, skew=0):
    waiting = list(chains)
    n_started = 0
    live = []
    turn = 0
    while waiting or live:
        while waiting and turn >= n_started * skew:
            live.append(waiting.pop(0))
            n_started += 1
        for c in list(live):
            try:
                next(c)
            except StopIteration:
                live.remove(c)
        turn += 1


def _load_weights_as_bf16(jobs, sem):
    chunks = []
    for src, dst, stage in jobs:
        rows = stage.shape[1]
        for r0 in range(0, src.shape[0], rows):
            chunks.append((src.at[pl.ds(r0, rows)], dst.at[pl.ds(r0, rows)], stage))

    def copy(k):
        src, _, stage = chunks[k]
        return pltpu.make_async_copy(src, stage.at[k % 2], sem.at[k % 2])

    copy(0).start()
    for k, (_, dst, stage) in enumerate(chunks):
        if k + 1 < len(chunks):
            copy(k + 1).start()
        copy(k).wait()
        dst[...] = stage[k % 2].astype(BF16)


def _is_first_step_2d():
    return jnp.logical_and(pl.program_id(0) == 0, pl.program_id(1) == 0)


def _swiglu_residual(x, g, wg_ref, wu_ref, wd_ref):
    u = _rms(x, g).astype(BF16)
    acc = None
    for c0, c1 in FF_CHUNKS:
        gate = _dot(u, wg_ref[:, c0:c1])
        up = _dot(u, wu_ref[:, c0:c1])
        yield
        act = (gate * jax.nn.sigmoid(gate) * up).astype(BF16)
        part = _dot(act, wd_ref[c0:c1, :])
        acc = part if acc is None else acc + part
        yield
    return x + 0.5 * acc


_HBM = pl.BlockSpec(memory_space=pl.ANY)


def _swiglu_weight_scratch(D):
    return [
        pltpu.VMEM((D, D_FF), BF16),
        pltpu.VMEM((D, D_FF), BF16),
        pltpu.VMEM((D_FF, D), BF16),
        pltpu.VMEM((2, _cast_block_rows(D, WEIGHT_CHUNKS), D_FF), F32),
        pltpu.VMEM((2, _cast_block_rows(D_FF, WEIGHT_CHUNKS), D), F32),
        pltpu.SemaphoreType.DMA((2,)),
    ]


def _cast_block_rows(n_rows, n_steps):
    rows = V7X_BF16_SUBLANES
    while n_rows % rows or n_rows // rows > n_steps:
        rows += V7X_BF16_SUBLANES
    return rows


def _cast_specs(to_cast, n_steps, step_of):
    specs = []
    for w in to_cast:
        rows = _cast_block_rows(w.shape[0], n_steps)
        last = w.shape[0] // rows - 1
        specs.append(pl.BlockSpec(
            (rows, w.shape[1]), lambda *idx, last=last: (jnp.minimum(step_of(*idx), last), 0)))
    return specs


def _cast_blocks(srcs, dsts):
    for src, dst in zip(srcs, dsts):
        dst[...] = src[...].astype(BF16)


def _ffn1_kernel(x_ref, n1_ref, wg_hbm, wu_hbm, wd_hbm, n2_ref, win_hbm, h_ref, p_ref, s_ref,
                 wg_ref, wu_ref, wd_ref, stage_up, stage_down, sem, wps_ref):
    @pl.when(_is_first_step_2d())
    def _():
        n_ps = D_POOL + D_SSM
        _load_weights_as_bf16([
            (wg_hbm, wg_ref, stage_up),
            (wu_hbm, wu_ref, stage_up),
            (wd_hbm, wd_ref, stage_down),
            (win_hbm.at[:, pl.ds(0, n_ps)], wps_ref, stage_up.at[:, :, pl.ds(0, n_ps)]),
        ], sem)

    def chain(rows):
        h = yield from _swiglu_residual(x_ref[rows, :], n1_ref[...], wg_ref, wu_ref, wd_ref)
        h_ref[rows, :] = h
        u = _rms(h, n2_ref[...]).astype(BF16)
        ps = _dot(u, wps_ref[...])
        p_ref[rows, :] = ps[:, :D_POOL]
        s_ref[rows, :] = ps[:, D_POOL:]

    _trace_round_robin([chain(rows) for rows in _row_groups(x_ref.shape[0])], skew=FFN_SKEW)


def _ffn1_call(x, n1, wg, wu, wd, n2, w_in):
    B, L, D = x.shape
    T = FFN_TILE
    tok = pl.BlockSpec((None, T, D), lambda b, i: (b, i, 0))
    return pl.pallas_call(
        _ffn1_kernel,
        grid=(B, L // T),
        in_specs=[tok, _const_spec((1, D)), _HBM, _HBM, _HBM, _const_spec((1, D)), _HBM],
        out_specs=[
            tok,
            pl.BlockSpec((None, T, D_POOL), lambda b, i: (b, i, 0)),
            pl.BlockSpec((None, T, D_SSM), lambda b, i: (b, i, 0)),
        ],
        out_shape=[
            jax.ShapeDtypeStruct((B, L, D), F32),
            jax.ShapeDtypeStruct((B, L, D_POOL), F32),
            jax.ShapeDtypeStruct((B, L, D_SSM), F32),
        ],
        scratch_shapes=_swiglu_weight_scratch(D) + [pltpu.VMEM((D, D_POOL + D_SSM), BF16)],
        compiler_params=pltpu.CompilerParams(
            dimension_semantics=("arbitrary", "arbitrary"), vmem_limit_bytes=V7X_VMEM_LIMIT_BYTES),
        name="ffn1",
    )(x, n1, wg, wu, wd, n2, w_in)


def _ffn2_kernel(h_ref, n_ref, wg_ref, wu_ref, wd_ref, fn_ref, o_ref):
    def chain(rows):
        h = yield from _swiglu_residual(h_ref[rows, :], n_ref[...], wg_ref, wu_ref, wd_ref)
        o_ref[rows, :] = _rms(h, fn_ref[...])

    _trace_round_robin([chain(rows) for rows in _row_groups(h_ref.shape[0])], skew=FFN_SKEW)


def _ffn2_call(h, n, wg, wu, wd, fn):
    B, L, D = h.shape
    T = FFN_TILE
    tok = pl.BlockSpec((None, T, D), lambda b, i: (b, i, 0))
    return pl.pallas_call(
        _ffn2_kernel,
        grid=(B, L // T),
        in_specs=[tok, _const_spec((1, D)), _const_spec((D, D_FF)), _const_spec((D, D_FF)),
                  _const_spec((D_FF, D)), _const_spec((1, D))],
        out_specs=tok,
        out_shape=jax.ShapeDtypeStruct((B, L, D), F32),
        compiler_params=pltpu.CompilerParams(
            dimension_semantics=("arbitrary", "arbitrary"), vmem_limit_bytes=V7X_VMEM_LIMIT_BYTES),
        name="ffn2",
    )(h, n, wg, wu, wd, fn)


def _ssm_kernel(*refs, n_cast):
    (sf_ref, sb_ref, are_ref, aim_ref, ldt_ref, bre_ref, bim_ref, crt_ref, cit_ref, d_ref), refs = refs[:10], refs[10:]
    cast_src, refs = refs[:n_cast], refs[n_cast:]
    (yf_ref, yb_ref), refs = refs[:2], refs[2:]
    cast_dst, refs = refs[:n_cast], refs[n_cast:]
    abar_scr, bmat_scr, cmat_scr, state_scr, bu_scr, xb_scr = refs
    i = pl.program_id(0)
    S = D_STATE
    n_b, steps, _ = sf_ref.shape
    R = steps * n_b

    @pl.when(i == 0)
    def _():
        for d in range(2):
            ar = are_ref[d]
            ai = aim_ref[d]
            dt = jnp.exp(ldt_ref[d])
            mag = jnp.exp(dt * ar)
            ang = dt * ai
            abr = mag * jnp.cos(ang)
            abi = mag * jnp.sin(ang)
            den = ar * ar + ai * ai
            nr = abr - 1.0
            qr = (nr * ar + abi * ai) / den
            qi = (abi * ar - nr * ai) / den
            abar_scr[d, 0] = jnp.broadcast_to(abr, (8, S))
            abar_scr[d, 1] = jnp.broadcast_to(abi, (8, S))
            br = bre_ref[d]
            bi = bim_ref[d]
            bmat_scr[d, :, :S] = (qr * br - qi * bi).astype(BF16)
            bmat_scr[d, :, S:] = (qr * bi + qi * br).astype(BF16)
            cmat_scr[d, :S, :] = crt_ref[d]
            cmat_scr[d, S:, :] = -cit_ref[d]
        state_scr[...] = jnp.zeros_like(state_scr)

    sub = SSM_SUBTILE
    n_sub = steps // sub
    halves = [(slice(c0, c0 + SCAN_LANES), slice(S + c0, S + c0 + SCAN_LANES))
              for c0 in range(0, S, SCAN_LANES)]
    def chain(d, s_ref, y_ref):
        order = list(range(n_sub)) if d == 0 else list(range(n_sub - 1, -1, -1))

        def input_map(j):
            s_nat = s_ref[:, j * sub:(j + 1) * sub, :]
            s_tb = jnp.swapaxes(s_nat, 0, 1).reshape(sub * n_b, D_SSM)
            bu_scr[d, j * sub * n_b:(j + 1) * sub * n_b, :] = _dot(s_tb.astype(BF16), bmat_scr[d])

        coef = [(abar_scr[d, 0, :, re], abar_scr[d, 1, :, re]) for re, _ in halves]
        state = [(state_scr[d, 0, :, re], state_scr[d, 1, :, re]) for re, _ in halves]
        input_map(order[0])
        yield
        for pos, j in enumerate(order):
            if pos + 1 < n_sub:
                input_map(order[pos + 1])
            for hf, (re, im) in enumerate(halves):
                ar, ai = coef[hf]
                xr, xi = state[hf]
                for k in range(0, sub, 2):
                    lo = j * sub + (k if d == 0 else sub - 2 - k)
                    got = {}
                    for t in ((lo, lo + 1) if d == 0 else (lo + 1, lo)):
                        rows = slice(n_b * t, n_b * (t + 1))
                        nxr = ar * xr - ai * xi + bu_scr[d, rows, re]
                        nxi = ar * xi + ai * xr + bu_scr[d, rows, im]
                        xr, xi = nxr, nxi
                        got[t] = (nxr, nxi)
                    rows2 = slice(n_b * lo, n_b * (lo + 2))
                    xb_scr[d, rows2, re] = jnp.concatenate([got[lo][0], got[lo + 1][0]], axis=0).astype(BF16)
                    xb_scr[d, rows2, im] = jnp.concatenate([got[lo][1], got[lo + 1][1]], axis=0).astype(BF16)
                state[hf] = (xr, xi)
            rows_j = slice(j * sub * n_b, (j + 1) * sub * n_b)
            y_tb = _dot(xb_scr[d, rows_j, :], cmat_scr[d])
            y = jnp.swapaxes(y_tb.reshape(sub, n_b, D_SSM), 0, 1)
            if d == 0:
                y = y + sf_ref[:, j * sub:(j + 1) * sub, :] * d_ref[...]
            y_ref[:, j * sub:(j + 1) * sub, :] = y
            yield
        for hf, (re, _) in enumerate(halves):
            state_scr[d, 0, :, re] = state[hf][0]
            state_scr[d, 1, :, re] = state[hf][1]

    _trace_round_robin([chain(0, sf_ref, yf_ref), chain(1, sb_ref, yb_ref)])
    _cast_blocks(cast_src, cast_dst)


def _ssm_call(s, a_re, a_im, log_dt, b_re_bd, b_im_bd, c_re_t, c_im_t, d_skip, to_cast):
    B, L, _ = s.shape
    assert B == V7X_SUBLANES, "the scan keeps one batch element per sublane"
    T = SSM_TILE
    n = L // T
    S = D_STATE
    fwd = pl.BlockSpec((B, T, D_SSM), lambda i: (0, i, 0))
    bwd = pl.BlockSpec((B, T, D_SSM), lambda i: (0, n - 1 - i, 0))
    cast_specs = _cast_specs(to_cast, n, lambda i: i)
    outs = pl.pallas_call(
        functools.partial(_ssm_kernel, n_cast=len(to_cast)),
        grid=(n,),
        in_specs=[
            fwd, bwd,
            _const_spec((2, 1, S)), _const_spec((2, 1, S)), _const_spec((2, 1, S)),
            _const_spec((2, D_SSM, S)), _const_spec((2, D_SSM, S)),
            _const_spec((2, S, D_SSM)), _const_spec((2, S, D_SSM)),
            _const_spec((1, D_SSM)),
        ] + cast_specs,
        out_specs=[fwd, bwd] + cast_specs,
        out_shape=[jax.ShapeDtypeStruct((B, L, D_SSM), F32)] * 2
                  + [jax.ShapeDtypeStruct(w.shape, BF16) for w in to_cast],
        scratch_shapes=[
            pltpu.VMEM((2, 2, 8, S), F32),
            pltpu.VMEM((2, D_SSM, 2 * S), BF16),
            pltpu.VMEM((2, 2 * S, D_SSM), BF16),
            pltpu.VMEM((2, 2, 8, S), F32),
            pltpu.VMEM((2, B * T, 2 * S), F32),
            pltpu.VMEM((2, B * T, 2 * S), BF16),
        ],
        compiler_params=pltpu.CompilerParams(
            dimension_semantics=("arbitrary",), vmem_limit_bytes=V7X_VMEM_LIMIT_BYTES),
        name="ssm_scan",
    )(s, s, a_re, a_im, log_dt, b_re_bd, b_im_bd, c_re_t, c_im_t, d_skip, *to_cast)
    return outs[0], outs[1], outs[2:]


def _window_sum(v, w):
    n = v.shape[0]
    ahead = lambda a, k: pltpu.roll(a, n - k, axis=0)
    behind = lambda a, k: pltpu.roll(a, k, axis=0)
    if w == 2:
        return v + behind(v, 1)
    cur, span = v + ahead(v, 1), 2
    while span < w // 2:
        cur, span = cur + ahead(cur, span), 2 * span
    return cur + behind(cur, span)


def _mix_kernel(h_ref, p_ref, pl_ref, pr_ref, yf_ref, yb_ref, k_ref, v_ref,
                nmix_ref, wgate_ref, poolw_ref, pscale_ref, wpp_ref, wgluv_ref, wglug_ref, wmo_ref,
                nx_ref, wq_ref, wxo_ref, o_ref, *, seq_len):
    i = pl.program_id(1)
    n_i = pl.num_programs(1)
    T = h_ref.shape[0]
    H = POOL_HALO

    halo_l = jnp.where(i > 0, pl_ref[...], 0.0)
    halo_r = jnp.where(i < n_i - 1, pr_ref[...], 0.0)
    edge = V7X_SUBLANES
    t_edge = lax.broadcasted_iota(jnp.int32, (edge, POOL_GROUP), 0)
    pooled_groups = []
    for g, w in enumerate(POOL_WINDOWS):
        left = w // 2
        right = w - 1 - left
        lanes = slice(g * POOL_GROUP, (g + 1) * POOL_GROUP)
        v = jnp.concatenate([halo_l[:, lanes], p_ref[:, lanes], halo_r[:, lanes]], axis=0)
        tot = _window_sum(v, w)[H:H + T]
        self_ = v[H:H + T]

        def clipped(t0, rows):
            t = i * T + t0 + t_edge
            cnt = jnp.minimum(t + (right + 1), seq_len) - jnp.maximum(t - left, 0)
            return tot[rows] / cnt.astype(F32) - self_[rows]

        pooled_groups.append(jnp.concatenate([
            clipped(0, slice(0, edge)),
            tot[edge:T - edge] * (1.0 / w) - self_[edge:T - edge],
            clipped(T - edge, slice(T - edge, T)),
        ], axis=0).astype(BF16))

    def chain(rows):
        h = h_ref[rows, :]
        mixed = [_dot(pooled_groups[g][rows], poolw_ref[g]) for g in range(len(POOL_WINDOWS))]
        yield
        mixed = jnp.concatenate(mixed, axis=-1) * pscale_ref[...]
        z_pool = _dot(mixed.astype(BF16), wpp_ref[...])
        yield

        y = jax.nn.gelu(yf_ref[rows, :] + yb_ref[rows, :]).astype(BF16)
        glu_val = _dot(y, wgluv_ref[...])
        glu_gate = _dot(y, wglug_ref[...])
        yield
        z_ssm = glu_val * jax.nn.sigmoid(glu_gate)

        u = _rms(h, nmix_ref[...]).astype(BF16)
        gates = _dot(u, wgate_ref[...])
        yield
        merged = (jax.nn.sigmoid(gates[:, :D_MODEL]) * z_pool
                  + jax.nn.sigmoid(gates[:, D_MODEL:]) * z_ssm)
        h = h + _dot(merged.astype(BF16), wmo_ref[...])
        yield

        q = _dot(_rms(h, nx_ref[...]).astype(BF16), wq_ref[...])
        yield
        head_cols = [slice(hd * XHEAD_DIM, (hd + 1) * XHEAD_DIM) for hd in range(N_XHEADS)]
        scores = [lax.dot_general(q[:, cols].astype(BF16), k_ref[:, cols], (((1,), (1,)), ((), ())),
                                  preferred_element_type=F32) / math.sqrt(XHEAD_DIM) for cols in head_cols]
        yield
        heads = []
        for sc, cols in zip(scores, head_cols):
            e = jnp.exp(sc - jnp.max(sc, axis=-1, keepdims=True))
            probs = e / jnp.sum(e, axis=-1, keepdims=True)
            heads.append(_dot(probs.astype(BF16), v_ref[:, cols]))
        yield
        o = jnp.concatenate(heads, axis=-1).astype(BF16)
        o_ref[rows, :] = h + _dot(o, wxo_ref[...])

    _trace_round_robin([chain(rows) for rows in _row_groups(T)], skew=MIX_SKEW)


def _mix_call(h, p, yf, yb, k, v, nmix, wgate, poolw, pscale, wpp, wgluv, wglug, wmo, nx, wq, wxo):
    B, L, D = h.shape
    T = MIX_TILE
    H = POOL_HALO
    per_tile = T // H
    n_halo = L // H
    tok = pl.BlockSpec((None, T, D), lambda b, i: (b, i, 0))
    y_spec = pl.BlockSpec((None, T, D_SSM), lambda b, i: (b, i, 0))
    mem_spec = pl.BlockSpec((None, N_MEM, D), lambda b, i: (b, 0, 0))
    return pl.pallas_call(
        functools.partial(_mix_kernel, seq_len=L),
        grid=(B, L // T),
        in_specs=[
            tok,
            pl.BlockSpec((None, T, D_POOL), lambda b, i: (b, i, 0)),
            pl.BlockSpec((None, H, D_POOL), lambda b, i: (b, jnp.maximum(i * per_tile - 1, 0), 0)),
            pl.BlockSpec((None, H, D_POOL), lambda b, i: (b, jnp.minimum((i + 1) * per_tile, n_halo - 1), 0)),
            y_spec, y_spec,
            mem_spec, mem_spec,
            _const_spec((1, D)),
            _const_spec((D, 2 * D)),
            _const_spec((len(POOL_WINDOWS), POOL_GROUP, POOL_GROUP)),
            _const_spec((1, D_POOL)),
            _const_spec((D_POOL, D)),
            _const_spec((D_SSM, D)),
            _const_spec((D_SSM, D)),
            _const_spec((D, D)),
            _const_spec((1, D)),
            _const_spec((D, D)),
            _const_spec((D, D)),
        ],
        out_specs=tok,
        out_shape=jax.ShapeDtypeStruct((B, L, D), F32),
        compiler_params=pltpu.CompilerParams(
            dimension_semantics=("arbitrary", "arbitrary"), vmem_limit_bytes=V7X_VMEM_LIMIT_BYTES),
        name="mix_attn",
    )(h, p, p, p, yf, yb, k, v, nmix, wgate, poolw, pscale, wpp, wgluv, wglug, wmo, nx, wq, wxo)


def _block_diag_in(w):
    G, P, Hh = w.shape
    eye = jnp.eye(G, dtype=w.dtype)
    return jnp.einsum("gph,gk->ghkp", w, eye).reshape(G * Hh, G * P)


def _block_diag_out(w):
    G, Hh, P = w.shape
    eye = jnp.eye(G, dtype=w.dtype)
    return jnp.einsum("ghp,gk->gpkh", w, eye).reshape(G * P, G * Hh)


def kernel(x, mem, ffn1_norm, ffn1_w_gate, ffn1_w_up, ffn1_w_down, mix_norm, w_in, pool_w, pool_scale,
           w_pool_proj, ssm_a_re, ssm_a_im, ssm_log_dt, ssm_b_re, ssm_b_im, ssm_c_re, ssm_c_im, ssm_d,
           w_glu_val, w_glu_gate, w_mix_out, xattn_norm, mem_norm, w_q, w_kv, w_xo, ffn2_norm,
           ffn2_w_gate, ffn2_w_up, ffn2_w_down, final_norm):
    B, L, D = x.shape
    depth = ffn1_norm.shape[0]
    row = lambda v: v.reshape(1, -1)
    bf = lambda w: w.astype(BF16)
    h = x
    for l in range(depth):
        k_mem, v_mem, cast = _kv_call(
            mem, row(mem_norm[l]), w_kv[l],
            [w_in[l], pool_w[l].reshape(-1, POOL_GROUP), w_pool_proj[l], w_glu_val[l], w_glu_gate[l],
             w_mix_out[l], w_q[l], w_xo[l]])
        w_in_b, pool_w_b, w_pp_b, w_gluv_b, w_glug_b, w_mo_b, w_q_b, w_xo_b = cast

        h, p, s = _ffn1_call(
            h, row(ffn1_norm[l]), ffn1_w_gate[l], ffn1_w_up[l], ffn1_w_down[l], row(mix_norm[l]), w_in[l])

        G, P = N_SSM_GROUPS, SSM_STATE
        per_state = lambda a: a.reshape(2, 1, G * P)
        log_dt = jnp.broadcast_to(ssm_log_dt[l][:, :, None], (2, G, P))
        yf, yb, (ffn2_wg_b, ffn2_wu_b, ffn2_wd_b) = _ssm_call(
            s, per_state(ssm_a_re[l]), per_state(ssm_a_im[l]), per_state(log_dt),
            jax.vmap(_block_diag_in)(ssm_b_re[l]), jax.vmap(_block_diag_in)(ssm_b_im[l]),
            bf(jax.vmap(_block_diag_out)(ssm_c_re[l])), bf(jax.vmap(_block_diag_out)(ssm_c_im[l])),
            row(ssm_d[l]), [ffn2_w_gate[l], ffn2_w_up[l], ffn2_w_down[l]])

        h = _mix_call(
            h, p, yf, yb, k_mem, v_mem,
            row(mix_norm[l]), w_in_b[:, D_POOL + D_SSM:], pool_w_b.reshape(pool_w[l].shape), row(pool_scale[l]),
            w_pp_b, w_gluv_b, w_glug_b, w_mo_b, row(xattn_norm[l]), w_q_b, w_xo_b)

        last = l == depth - 1
        assert last, "deeper stacks need a non-final variant of the second SwiGLU kernel"
        h = _ffn2_call(h, row(ffn2_norm[l]), ffn2_wg_b, ffn2_wu_b, ffn2_wd_b, row(final_norm))
    return h
```

```python
import functools
import math

import jax
import jax.numpy as jnp
from jax import lax
from jax.experimental import pallas as pl
from jax.experimental.pallas import tpu as pltpu

F32 = jnp.float32
BF16 = jnp.bfloat16

D_MODEL = 1024
N_MEM = 256
D_FF = 2816
D_POOL = 512
POOL_WINDOWS = (2, 4, 8, 16)
POOL_GROUP = 128
D_SSM = 256
SSM_GROUP = 16
N_SSM_GROUPS = 16
SSM_STATE = 64
D_STATE = N_SSM_GROUPS * SSM_STATE
N_XHEADS = 4
XHEAD_DIM = 256
EPS = 1e-6

V7X_VMEM_LIMIT_BYTES = 58 * 1024 * 1024
V7X_SUBLANES = 8
V7X_BF16_SUBLANES = 16
POOL_HALO = 16

FFN_TILE = 1024
MIX_TILE = 1024
SSM_TILE = 128
FFN_SKEW = 1
MIX_SKEW = 1
WEIGHT_CHUNKS = 16
WEIGHT_SLOTS = 4
GROUP_ROWS = 256
FF_CHUNKS = ((0, 768), (768, 1536), (1536, 2304), (2304, 2816))
SSM_SUBTILE = 32
SCAN_LANES = 512


def _rms(v, g):
    r = lax.rsqrt(jnp.mean(v * v, axis=-1, keepdims=True) + EPS)
    return (v * r) * g


def _dot(a, b):
    return jnp.dot(a, b, preferred_element_type=F32)


def _const_spec(shape):
    nd = len(shape)
    return pl.BlockSpec(shape, lambda *_: (0,) * nd)


def _kv_kernel(*refs, n_cast):
    (mem_ref, g_ref, wkv_ref), refs = refs[:3], refs[3:]
    cast_src, refs = refs[:n_cast], refs[n_cast:]
    (k_ref, v_ref), refs = refs[:2], refs[2:]
    cast_dst, (wkv_bf16,) = refs[:n_cast], refs[n_cast:]

    @pl.when(pl.program_id(0) == 0)
    def _():
        wkv_bf16[...] = wkv_ref[...].astype(BF16)

    m = _rms(mem_ref[...], g_ref[...]).astype(BF16)
    kv = _dot(m, wkv_bf16[...])
    k_ref[...] = kv[:, :D_MODEL].astype(BF16)
    v_ref[...] = kv[:, D_MODEL:].astype(BF16)
    _cast_blocks(cast_src, cast_dst)


def _kv_call(mem, mem_norm, w_kv, to_cast):
    B = mem.shape[0]
    per_batch = pl.BlockSpec((None, N_MEM, D_MODEL), lambda b: (b, 0, 0))
    cast_specs = _cast_specs(to_cast, B, lambda b: b)
    outs = pl.pallas_call(
        functools.partial(_kv_kernel, n_cast=len(to_cast)),
        grid=(B,),
        in_specs=[per_batch, _const_spec((1, D_MODEL)), _const_spec((D_MODEL, 2 * D_MODEL))] + cast_specs,
        out_specs=[per_batch, per_batch] + cast_specs,
        out_shape=[jax.ShapeDtypeStruct((B, N_MEM, D_MODEL), BF16)] * 2
                  + [jax.ShapeDtypeStruct(w.shape, BF16) for w in to_cast],
        scratch_shapes=[pltpu.VMEM((D_MODEL, 2 * D_MODEL), BF16)],
        compiler_params=pltpu.CompilerParams(
            dimension_semantics=("arbitrary",), vmem_limit_bytes=V7X_VMEM_LIMIT_BYTES),
        name="kv_proj",
    )(mem, mem_norm, w_kv, *to_cast)
    return outs[0], outs[1], outs[2:]


def _row_groups(n_rows):
    return [slice(r, r + GROUP_ROWS) for r in range(0, n_rows, GROUP_ROWS)]


def _trace_round_robin(chains, skew=0):
    waiting = list(chains)
    n_started = 0
    live = []
    turn = 0
    while waiting or live:
        while waiting and turn >= n_started * skew:
            live.append(waiting.pop(0))
            n_started += 1
        for c in list(live):
            try:
                next(c)
            except StopIteration:
                live.remove(c)
        turn += 1


def _load_weights_as_bf16(jobs, sem):
    chunks = []
    for src, dst, stage in jobs:
        rows = stage.shape[1]
        for r0 in range(0, src.shape[0], rows):
            chunks.append((src.at[pl.ds(r0, rows)], dst.at[pl.ds(r0, rows)], stage))
    ahead = WEIGHT_SLOTS

    def copy(k):
        src, _, stage = chunks[k]
        return pltpu.make_async_copy(src, stage.at[k % WEIGHT_SLOTS], sem.at[k % WEIGHT_SLOTS])

    for k in range(min(ahead, len(chunks))):
        copy(k).start()
    for k, (_, dst, stage) in enumerate(chunks):
        copy(k).wait()
        dst[...] = stage[k % WEIGHT_SLOTS].astype(BF16)
        if k + ahead < len(chunks):
            copy(k + ahead).start()


def _is_first_step_2d():
    return jnp.logical_and(pl.program_id(0) == 0, pl.program_id(1) == 0)


def _swiglu_residual(x, g, wg_ref, wu_ref, wd_ref):
    u = _rms(x, g).astype(BF16)
    acc = None
    for c0, c1 in FF_CHUNKS:
        gate = _dot(u, wg_ref[:, c0:c1])
        up = _dot(u, wu_ref[:, c0:c1])
        yield
        act = (gate * jax.nn.sigmoid(gate) * up).astype(BF16)
        part = _dot(act, wd_ref[c0:c1, :])
        acc = part if acc is None else acc + part
        yield
    return x + 0.5 * acc


_HBM = pl.BlockSpec(memory_space=pl.ANY)


def _swiglu_weight_scratch(D):
    return [
        pltpu.VMEM((D, D_FF), BF16),
        pltpu.VMEM((D, D_FF), BF16),
        pltpu.VMEM((D_FF, D), BF16),
        pltpu.VMEM((WEIGHT_SLOTS, _cast_block_rows(D, WEIGHT_CHUNKS), D_FF), F32),
        pltpu.VMEM((WEIGHT_SLOTS, _cast_block_rows(D_FF, WEIGHT_CHUNKS), D), F32),
        pltpu.SemaphoreType.DMA((WEIGHT_SLOTS,)),
    ]


def _cast_block_rows(n_rows, n_steps):
    rows = V7X_BF16_SUBLANES
    while n_rows % rows or n_rows // rows > n_steps:
        rows += V7X_BF16_SUBLANES
    return rows


def _cast_specs(to_cast, n_steps, step_of):
    specs = []
    for w in to_cast:
        rows = _cast_block_rows(w.shape[0], n_steps)
        last = w.shape[0] // rows - 1
        specs.append(pl.BlockSpec(
            (rows, w.shape[1]), lambda *idx, last=last: (jnp.minimum(step_of(*idx), last), 0)))
    return specs


def _cast_blocks(srcs, dsts):
    for src, dst in zip(srcs, dsts):
        dst[...] = src[...].astype(BF16)


def _ffn1_kernel(x_ref, n1_ref, wg_hbm, wu_hbm, wd_hbm, n2_ref, win_hbm, h_ref, p_ref, s_ref,
                 wg_ref, wu_ref, wd_ref, stage_up, stage_down, sem, wps_ref):
    @pl.when(_is_first_step_2d())
    def _():
        n_ps = D_POOL + D_SSM
        _load_weights_as_bf16([
            (wg_hbm, wg_ref, stage_up),
            (wu_hbm, wu_ref, stage_up),
            (wd_hbm, wd_ref, stage_down),
            (win_hbm.at[:, pl.ds(0, n_ps)], wps_ref, stage_up.at[:, :, pl.ds(0, n_ps)]),
        ], sem)

    def chain(rows):
        h = yield from _swiglu_residual(x_ref[rows, :], n1_ref[...], wg_ref, wu_ref, wd_ref)
        h_ref[rows, :] = h
        u = _rms(h, n2_ref[...]).astype(BF16)
        ps = _dot(u, wps_ref[...])
        p_ref[rows, :] = ps[:, :D_POOL]
        s_ref[rows, :] = ps[:, D_POOL:]

    _trace_round_robin([chain(rows) for rows in _row_groups(x_ref.shape[0])], skew=FFN_SKEW)


def _ffn1_call(x, n1, wg, wu, wd, n2, w_in):
    B, L, D = x.shape
    T = FFN_TILE
    tok = pl.BlockSpec((None, T, D), lambda b, i: (b, i, 0))
    return pl.pallas_call(
        _ffn1_kernel,
        grid=(B, L // T),
        in_specs=[tok, _const_spec((1, D)), _HBM, _HBM, _HBM, _const_spec((1, D)), _HBM],
        out_specs=[
            tok,
            pl.BlockSpec((None, T, D_POOL), lambda b, i: (b, i, 0)),
            pl.BlockSpec((None, T, D_SSM), lambda b, i: (b, i, 0)),
        ],
        out_shape=[
            jax.ShapeDtypeStruct((B, L, D), F32),
            jax.ShapeDtypeStruct((B, L, D_POOL), F32),
            jax.ShapeDtypeStruct((B, L, D_SSM), F32),
        ],
        scratch_shapes=_swiglu_weight_scratch(D) + [pltpu.VMEM((D, D_POOL + D_SSM), BF16)],
        compiler_params=pltpu.CompilerParams(
            dimension_semantics=("arbitrary", "arbitrary"), vmem_limit_bytes=V7X_VMEM_LIMIT_BYTES),
        name="ffn1",
    )(x, n1, wg, wu, wd, n2, w_in)


def _ffn2_kernel(h_ref, n_ref, wg_ref, wu_ref, wd_ref, fn_ref, o_ref):
    def chain(rows):
        h = yield from _swiglu_residual(h_ref[rows, :], n_ref[...], wg_ref, wu_ref, wd_ref)
        o_ref[rows, :] = _rms(h, fn_ref[...])

    _trace_round_robin([chain(rows) for rows in _row_groups(h_ref.shape[0])], skew=FFN_SKEW)


def _ffn2_call(h, n, wg, wu, wd, fn):
    B, L, D = h.shape
    T = FFN_TILE
    tok = pl.BlockSpec((None, T, D), lambda b, i: (b, i, 0))
    return pl.pallas_call(
        _ffn2_kernel,
        grid=(B, L // T),
        in_specs=[tok, _const_spec((1, D)), _const_spec((D, D_FF)), _const_spec((D, D_FF)),
                  _const_spec((D_FF, D)), _const_spec((1, D))],
        out_specs=tok,
        out_shape=jax.ShapeDtypeStruct((B, L, D), F32),
        compiler_params=pltpu.CompilerParams(
            dimension_semantics=("arbitrary", "arbitrary"), vmem_limit_bytes=V7X_VMEM_LIMIT_BYTES),
        name="ffn2",
    )(h, n, wg, wu, wd, fn)


def _ssm_kernel(*refs, n_cast):
    (sf_ref, sb_ref, are_ref, aim_ref, ldt_ref, bre_ref, bim_ref, crt_ref, cit_ref, d_ref), refs = refs[:10], refs[10:]
    cast_src, refs = refs[:n_cast], refs[n_cast:]
    (yf_ref, yb_ref), refs = refs[:2], refs[2:]
    cast_dst, refs = refs[:n_cast], refs[n_cast:]
    abar_scr, bmat_scr, cmat_scr, state_scr, bu_scr, xb_scr = refs
    i = pl.program_id(0)
    S = D_STATE
    n_b, steps, _ = sf_ref.shape
    R = steps * n_b

    @pl.when(i == 0)
    def _():
        for d in range(2):
            ar = are_ref[d]
            ai = aim_ref[d]
            dt = jnp.exp(ldt_ref[d])
            mag = jnp.exp(dt * ar)
            ang = dt * ai
            abr = mag * jnp.cos(ang)
            abi = mag * jnp.sin(ang)
            den = ar * ar + ai * ai
            nr = abr - 1.0
            qr = (nr * ar + abi * ai) / den
            qi = (abi * ar - nr * ai) / den
            abar_scr[d, 0] = jnp.broadcast_to(abr, (8, S))
            abar_scr[d, 1] = jnp.broadcast_to(abi, (8, S))
            br = bre_ref[d]
            bi = bim_ref[d]
            bmat_scr[d, :, :S] = (qr * br - qi * bi).astype(BF16)
            bmat_scr[d, :, S:] = (qr * bi + qi * br).astype(BF16)
            cmat_scr[d, :S, :] = crt_ref[d]
            cmat_scr[d, S:, :] = -cit_ref[d]
        state_scr[...] = jnp.zeros_like(state_scr)

    sub = SSM_SUBTILE
    n_sub = steps // sub
    halves = [(slice(c0, c0 + SCAN_LANES), slice(S + c0, S + c0 + SCAN_LANES))
              for c0 in range(0, S, SCAN_LANES)]
    def chain(d, s_ref, y_ref):
        order = list(range(n_sub)) if d == 0 else list(range(n_sub - 1, -1, -1))

        def input_map(j):
            s_nat = s_ref[:, j * sub:(j + 1) * sub, :]
            s_tb = jnp.swapaxes(s_nat, 0, 1).reshape(sub * n_b, D_SSM)
            bu_scr[d, j * sub * n_b:(j + 1) * sub * n_b, :] = _dot(s_tb.astype(BF16), bmat_scr[d])

        coef = [(abar_scr[d, 0, :, re], abar_scr[d, 1, :, re]) for re, _ in halves]
        state = [(state_scr[d, 0, :, re], state_scr[d, 1, :, re]) for re, _ in halves]
        input_map(order[0])
        yield
        for pos, j in enumerate(order):
            if pos + 1 < n_sub:
                input_map(order[pos + 1])
            for hf, (re, im) in enumerate(halves):
                ar, ai = coef[hf]
                xr, xi = state[hf]
                for k in range(0, sub, 2):
                    lo = j * sub + (k if d == 0 else sub - 2 - k)
                    got = {}
                    for t in ((lo, lo + 1) if d == 0 else (lo + 1, lo)):
                        rows = slice(n_b * t, n_b * (t + 1))
                        nxr = ar * xr - ai * xi + bu_scr[d, rows, re]
                        nxi = ar * xi + ai * xr + bu_scr[d, rows, im]
                        xr, xi = nxr, nxi
                        got[t] = (nxr, nxi)
                    rows2 = slice(n_b * lo, n_b * (lo + 2))
                    xb_scr[d, rows2, re] = jnp.concatenate([got[lo][0], got[lo + 1][0]], axis=0).astype(BF16)
                    xb_scr[d, rows2, im] = jnp.concatenate([got[lo][1], got[lo + 1][1]], axis=0).astype(BF16)
                state[hf] = (xr, xi)
            rows_j = slice(j * sub * n_b, (j + 1) * sub * n_b)
            y_tb = _dot(xb_scr[d, rows_j, :], cmat_scr[d])
            y = jnp.swapaxes(y_tb.reshape(sub, n_b, D_SSM), 0, 1)
            if d == 0:
                y = y + sf_ref[:, j * sub:(j + 1) * sub, :] * d_ref[...]
            y_ref[:, j * sub:(j + 1) * sub, :] = y
            yield
        for hf, (re, _) in enumerate(halves):
            state_scr[d, 0, :, re] = state[hf][0]
            state_scr[d, 1, :, re] = state[hf][1]

    _trace_round_robin([chain(0, sf_ref, yf_ref), chain(1, sb_ref, yb_ref)])
    _cast_blocks(cast_src, cast_dst)


def _ssm_call(s, a_re, a_im, log_dt, b_re_bd, b_im_bd, c_re_t, c_im_t, d_skip, to_cast):
    B, L, _ = s.shape
    assert B == V7X_SUBLANES, "the scan keeps one batch element per sublane"
    T = SSM_TILE
    n = L // T
    S = D_STATE
    fwd = pl.BlockSpec((B, T, D_SSM), lambda i: (0, i, 0))
    bwd = pl.BlockSpec((B, T, D_SSM), lambda i: (0, n - 1 - i, 0))
    cast_specs = _cast_specs(to_cast, n, lambda i: i)
    outs = pl.pallas_call(
        functools.partial(_ssm_kernel, n_cast=len(to_cast)),
        grid=(n,),
        in_specs=[
            fwd, bwd,
            _const_spec((2, 1, S)), _const_spec((2, 1, S)), _const_spec((2, 1, S)),
            _const_spec((2, D_SSM, S)), _const_spec((2, D_SSM, S)),
            _const_spec((2, S, D_SSM)), _const_spec((2, S, D_SSM)),
            _const_spec((1, D_SSM)),
        ] + cast_specs,
        out_specs=[fwd, bwd] + cast_specs,
        out_shape=[jax.ShapeDtypeStruct((B, L, D_SSM), F32)] * 2
                  + [jax.ShapeDtypeStruct(w.shape, BF16) for w in to_cast],
        scratch_shapes=[
            pltpu.VMEM((2, 2, 8, S), F32),
            pltpu.VMEM((2, D_SSM, 2 * S), BF16),
            pltpu.VMEM((2, 2 * S, D_SSM), BF16),
            pltpu.VMEM((2, 2, 8, S), F32),
            pltpu.VMEM((2, B * T, 2 * S), F32),
            pltpu.VMEM((2, B * T, 2 * S), BF16),
        ],
        compiler_params=pltpu.CompilerParams(
            dimension_semantics=("arbitrary",), vmem_limit_bytes=V7X_VMEM_LIMIT_BYTES),
        name="ssm_scan",
    )(s, s, a_re, a_im, log_dt, b_re_bd, b_im_bd, c_re_t, c_im_t, d_skip, *to_cast)
    return outs[0], outs[1], outs[2:]


def _window_sum(v, w):
    n = v.shape[0]
    ahead = lambda a, k: pltpu.roll(a, n - k, axis=0)
    behind = lambda a, k: pltpu.roll(a, k, axis=0)
    if w == 2:
        return v + behind(v, 1)
    cur, span = v + ahead(v, 1), 2
    while span < w // 2:
        cur, span = cur + ahead(cur, span), 2 * span
    return cur + behind(cur, span)


def _mix_kernel(h_ref, p_ref, pl_ref, pr_ref, yf_ref, yb_ref, k_ref, v_ref,
                nmix_ref, wgate_ref, poolw_ref, pscale_ref, wpp_ref, wgluv_ref, wglug_ref, wmo_ref,
                nx_ref, wq_ref, wxo_ref, o_ref, *, seq_len):
    i = pl.program_id(1)
    n_i = pl.num_programs(1)
    T = h_ref.shape[0]
    H = POOL_HALO

    halo_l = jnp.where(i > 0, pl_ref[...], 0.0)
    halo_r = jnp.where(i < n_i - 1, pr_ref[...], 0.0)
    edge = V7X_SUBLANES
    t_edge = lax.broadcasted_iota(jnp.int32, (edge, POOL_GROUP), 0)
    pooled_groups = []
    for g, w in enumerate(POOL_WINDOWS):
        left = w // 2
        right = w - 1 - left
        lanes = slice(g * POOL_GROUP, (g + 1) * POOL_GROUP)
        v = jnp.concatenate([halo_l[:, lanes], p_ref[:, lanes], halo_r[:, lanes]], axis=0)
        tot = _window_sum(v, w)[H:H + T]
        self_ = v[H:H + T]

        def clipped(t0, rows):
            t = i * T + t0 + t_edge
            cnt = jnp.minimum(t + (right + 1), seq_len) - jnp.maximum(t - left, 0)
            return tot[rows] / cnt.astype(F32) - self_[rows]

        pooled_groups.append(jnp.concatenate([
            clipped(0, slice(0, edge)),
            tot[edge:T - edge] * (1.0 / w) - self_[edge:T - edge],
            clipped(T - edge, slice(T - edge, T)),
        ], axis=0).astype(BF16))

    def chain(rows):
        h = h_ref[rows, :]
        mixed = [_dot(pooled_groups[g][rows], poolw_ref[g]) for g in range(len(POOL_WINDOWS))]
        yield
        mixed = jnp.concatenate(mixed, axis=-1) * pscale_ref[...]
        z_pool = _dot(mixed.astype(BF16), wpp_ref[...])
        yield

        y = jax.nn.gelu(yf_ref[rows, :] + yb_ref[rows, :]).astype(BF16)
        glu_val = _dot(y, wgluv_ref[...])
        glu_gate = _dot(y, wglug_ref[...])
        yield
        z_ssm = glu_val * jax.nn.sigmoid(glu_gate)

        u = _rms(h, nmix_ref[...]).astype(BF16)
        gates = _dot(u, wgate_ref[...])
        yield
        merged = (jax.nn.sigmoid(gates[:, :D_MODEL]) * z_pool
                  + jax.nn.sigmoid(gates[:, D_MODEL:]) * z_ssm)
        h = h + _dot(merged.astype(BF16), wmo_ref[...])
        yield

        q = _dot(_rms(h, nx_ref[...]).astype(BF16), wq_ref[...])
        yield
        head_cols = [slice(hd * XHEAD_DIM, (hd + 1) * XHEAD_DIM) for hd in range(N_XHEADS)]
        scores = [lax.dot_general(q[:, cols].astype(BF16), k_ref[:, cols], (((1,), (1,)), ((), ())),
                                  preferred_element_type=F32) / math.sqrt(XHEAD_DIM) for cols in head_cols]
        yield
        heads = []
        for sc, cols in zip(scores, head_cols):
            e = jnp.exp(sc - jnp.max(sc, axis=-1, keepdims=True))
            probs = e / jnp.sum(e, axis=-1, keepdims=True)
            heads.append(_dot(probs.astype(BF16), v_ref[:, cols]))
        yield
        o = jnp.concatenate(heads, axis=-1).astype(BF16)
        o_ref[rows, :] = h + _dot(o, wxo_ref[...])

    _trace_round_robin([chain(rows) for rows in _row_groups(T)], skew=MIX_SKEW)


def _mix_call(h, p, yf, yb, k, v, nmix, wgate, poolw, pscale, wpp, wgluv, wglug, wmo, nx, wq, wxo):
    B, L, D = h.shape
    T = MIX_TILE
    H = POOL_HALO
    per_tile = T // H
    n_halo = L // H
    tok = pl.BlockSpec((None, T, D), lambda b, i: (b, i, 0))
    y_spec = pl.BlockSpec((None, T, D_SSM), lambda b, i: (b, i, 0))
    mem_spec = pl.BlockSpec((None, N_MEM, D), lambda b, i: (b, 0, 0))
    return pl.pallas_call(
        functools.partial(_mix_kernel, seq_len=L),
        grid=(B, L // T),
        in_specs=[
            tok,
            pl.BlockSpec((None, T, D_POOL), lambda b, i: (b, i, 0)),
            pl.BlockSpec((None, H, D_POOL), lambda b, i: (b, jnp.maximum(i * per_tile - 1, 0), 0)),
            pl.BlockSpec((None, H, D_POOL), lambda b, i: (b, jnp.minimum((i + 1) * per_tile, n_halo - 1), 0)),
            y_spec, y_spec,
            mem_spec, mem_spec,
            _const_spec((1, D)),
            _const_spec((D, 2 * D)),
            _const_spec((len(POOL_WINDOWS), POOL_GROUP, POOL_GROUP)),
            _const_spec((1, D_POOL)),
            _const_spec((D_POOL, D)),
            _const_spec((D_SSM, D)),
            _const_spec((D_SSM, D)),
            _const_spec((D, D)),
            _const_spec((1, D)),
            _const_spec((D, D)),
            _const_spec((D, D)),
        ],
        out_specs=tok,
        out_shape=jax.ShapeDtypeStruct((B, L, D), F32),
        compiler_params=pltpu.CompilerParams(
            dimension_semantics=("arbitrary", "arbitrary"), vmem_limit_bytes=V7X_VMEM_LIMIT_BYTES),
        name="mix_attn",
    )(h, p, p, p, yf, yb, k, v, nmix, wgate, poolw, pscale, wpp, wgluv, wglug, wmo, nx, wq, wxo)


def _block_diag_in(w):
    G, P, Hh = w.shape
    eye = jnp.eye(G, dtype=w.dtype)
    return jnp.einsum("gph,gk->ghkp", w, eye).reshape(G * Hh, G * P)


def _block_diag_out(w):
    G, Hh, P = w.shape
    eye = jnp.eye(G, dtype=w.dtype)
    return jnp.einsum("ghp,gk->gpkh", w, eye).reshape(G * P, G * Hh)


def kernel(x, mem, ffn1_norm, ffn1_w_gate, ffn1_w_up, ffn1_w_down, mix_norm, w_in, pool_w, pool_scale,
           w_pool_proj, ssm_a_re, ssm_a_im, ssm_log_dt, ssm_b_re, ssm_b_im, ssm_c_re, ssm_c_im, ssm_d,
           w_glu_val, w_glu_gate, w_mix_out, xattn_norm, mem_norm, w_q, w_kv, w_xo, ffn2_norm,
           ffn2_w_gate, ffn2_w_up, ffn2_w_down, final_norm):
    B, L, D = x.shape
    depth = ffn1_norm.shape[0]
    row = lambda v: v.reshape(1, -1)
    bf = lambda w: w.astype(BF16)
    h = x
    for l in range(depth):
        k_mem, v_mem, cast = _kv_call(
            mem, row(mem_norm[l]), w_kv[l],
            [w_in[l], pool_w[l].reshape(-1, POOL_GROUP), w_pool_proj[l], w_glu_val[l], w_glu_gate[l],
             w_mix_out[l], w_q[l], w_xo[l]])
        w_in_b, pool_w_b, w_pp_b, w_gluv_b, w_glug_b, w_mo_b, w_q_b, w_xo_b = cast

        h, p, s = _ffn1_call(
            h, row(ffn1_norm[l]), ffn1_w_gate[l], ffn1_w_up[l], ffn1_w_down[l], row(mix_norm[l]), w_in[l])

        G, P = N_SSM_GROUPS, SSM_STATE
        per_state = lambda a: a.reshape(2, 1, G * P)
        log_dt = jnp.broadcast_to(ssm_log_dt[l][:, :, None], (2, G, P))
        yf, yb, (ffn2_wg_b, ffn2_wu_b, ffn2_wd_b) = _ssm_call(
            s, per_state(ssm_a_re[l]), per_state(ssm_a_im[l]), per_state(log_dt),
            jax.vmap(_block_diag_in)(ssm_b_re[l]), jax.vmap(_block_diag_in)(ssm_b_im[l]),
            bf(jax.vmap(_block_diag_out)(ssm_c_re[l])), bf(jax.vmap(_block_diag_out)(ssm_c_im[l])),
            row(ssm_d[l]), [ffn2_w_gate[l], ffn2_w_up[l], ffn2_w_down[l]])

        h = _mix_call(
            h, p, yf, yb, k_mem, v_mem,
            row(mix_norm[l]), w_in_b[:, D_POOL + D_SSM:], pool_w_b.reshape(pool_w[l].shape), row(pool_scale[l]),
            w_pp_b, w_gluv_b, w_glug_b, w_mo_b, row(xattn_norm[l]), w_q_b, w_xo_b)

        last = l == depth - 1
        assert last, "deeper stacks need a non-final variant of the second SwiGLU kernel"
        h = _ffn2_call(h, row(ffn2_norm[l]), ffn2_wg_b, ffn2_wu_b, ffn2_wd_b, row(final_norm))
    return h
```

```python
import functools
import math

import jax
import jax.numpy as jnp
from jax import lax
from jax.experimental import pallas as pl
from jax.experimental.pallas import tpu as pltpu

F32 = jnp.float32
BF16 = jnp.bfloat16

D_MODEL = 1024
N_MEM = 256
D_FF = 2816
D_POOL = 512
POOL_WINDOWS = (2, 4, 8, 16)
POOL_GROUP = 128
D_SSM = 256
SSM_GROUP = 16
N_SSM_GROUPS = 16
SSM_STATE = 64
D_STATE = N_SSM_GROUPS * SSM_STATE
N_XHEADS = 4
XHEAD_DIM = 256
EPS = 1e-6

V7X_VMEM_LIMIT_BYTES = 58 * 1024 * 1024
V7X_SUBLANES = 8
V7X_BF16_SUBLANES = 16
POOL_HALO = 16

FFN_TILE = 1024
MIX_TILE = 1024
SSM_TILE = 128
FFN_SKEW = 1
MIX_SKEW = 1
WEIGHT_CHUNKS = 16
WEIGHT_SLOTS = 4
GROUP_ROWS = 256
FF_CHUNKS = ((0, 768), (768, 1536), (1536, 2304), (2304, 2816))
SSM_AHEAD = 2
SSM_SUBTILE = 16
SCAN_LANES = 512


def _rms(v, g):
    r = lax.rsqrt(jnp.mean(v * v, axis=-1, keepdims=True) + EPS)
    return (v * r) * g


def _dot(a, b):
    return jnp.dot(a, b, preferred_element_type=F32)


def _const_spec(shape):
    nd = len(shape)
    return pl.BlockSpec(shape, lambda *_: (0,) * nd)


def _kv_kernel(*refs, n_cast):
    (mem_ref, g_ref, wkv_ref), refs = refs[:3], refs[3:]
    cast_src, refs = refs[:n_cast], refs[n_cast:]
    (k_ref, v_ref), refs = refs[:2], refs[2:]
    cast_dst, (wkv_bf16,) = refs[:n_cast], refs[n_cast:]

    @pl.when(pl.program_id(0) == 0)
    def _():
        wkv_bf16[...] = wkv_ref[...].astype(BF16)

    m = _rms(mem_ref[...], g_ref[...]).astype(BF16)
    kv = _dot(m, wkv_bf16[...])
    k_ref[...] = kv[:, :D_MODEL].astype(BF16)
    v_ref[...] = kv[:, D_MODEL:].astype(BF16)
    _cast_blocks(cast_src, cast_dst)


def _kv_call(mem, mem_norm, w_kv, to_cast):
    B = mem.shape[0]
    per_batch = pl.BlockSpec((None, N_MEM, D_MODEL), lambda b: (b, 0, 0))
    cast_specs = _cast_specs(to_cast, B, lambda b: b)
    outs = pl.pallas_call(
        functools.partial(_kv_kernel, n_cast=len(to_cast)),
        grid=(B,),
        in_specs=[per_batch, _const_spec((1, D_MODEL)), _const_spec((D_MODEL, 2 * D_MODEL))] + cast_specs,
        out_specs=[per_batch, per_batch] + cast_specs,
        out_shape=[jax.ShapeDtypeStruct((B, N_MEM, D_MODEL), BF16)] * 2
                  + [jax.ShapeDtypeStruct(w.shape, BF16) for w in to_cast],
        scratch_shapes=[pltpu.VMEM((D_MODEL, 2 * D_MODEL), BF16)],
        compiler_params=pltpu.CompilerParams(
            dimension_semantics=("arbitrary",), vmem_limit_bytes=V7X_VMEM_LIMIT_BYTES),
        name="kv_proj",
    )(mem, mem_norm, w_kv, *to_cast)
    return outs[0], outs[1], outs[2:]


def _row_groups(n_rows):
    return [slice(r, r + GROUP_ROWS) for r in range(0, n_rows, GROUP_ROWS)]


def _trace_round_robin(chains, skew=0):
    waiting = list(chains)
    n_started = 0
    live = []
    turn = 0
    while waiting or live:
        while waiting and turn >= n_started * skew:
            live.append(waiting.pop(0))
            n_started += 1
        for c in list(live):
            try:
                next(c)
            except StopIteration:
                live.remove(c)
        turn += 1


def _load_weights_as_bf16(jobs, sem):
    chunks = []
    for src, dst, stage in jobs:
        rows = stage.shape[1]
        for r0 in range(0, src.shape[0], rows):
            chunks.append((src.at[pl.ds(r0, rows)], dst.at[pl.ds(r0, rows)], stage))
    ahead = WEIGHT_SLOTS

    def copy(k):
        src, _, stage = chunks[k]
        return pltpu.make_async_copy(src, stage.at[k % WEIGHT_SLOTS], sem.at[k % WEIGHT_SLOTS])

    for k in range(min(ahead, len(chunks))):
        copy(k).start()
    for k, (_, dst, stage) in enumerate(chunks):
        copy(k).wait()
        dst[...] = stage[k % WEIGHT_SLOTS].astype(BF16)
        if k + ahead < len(chunks):
            copy(k + ahead).start()


def _is_first_step_2d():
    return jnp.logical_and(pl.program_id(0) == 0, pl.program_id(1) == 0)


def _swiglu_residual(x, g, wg_ref, wu_ref, wd_ref):
    u = _rms(x, g).astype(BF16)
    acc = None
    for c0, c1 in FF_CHUNKS:
        gate = _dot(u, wg_ref[:, c0:c1])
        up = _dot(u, wu_ref[:, c0:c1])
        yield
        act = (gate * jax.nn.sigmoid(gate) * up).astype(BF16)
        part = _dot(act, wd_ref[c0:c1, :])
        acc = part if acc is None else acc + part
        yield
    return x + 0.5 * acc


_HBM = pl.BlockSpec(memory_space=pl.ANY)


def _swiglu_weight_scratch(D):
    return [
        pltpu.VMEM((D, D_FF), BF16),
        pltpu.VMEM((D, D_FF), BF16),
        pltpu.VMEM((D_FF, D), BF16),
        pltpu.VMEM((WEIGHT_SLOTS, _cast_block_rows(D, WEIGHT_CHUNKS), D_FF), F32),
        pltpu.VMEM((WEIGHT_SLOTS, _cast_block_rows(D_FF, WEIGHT_CHUNKS), D), F32),
        pltpu.SemaphoreType.DMA((WEIGHT_SLOTS,)),
    ]


def _cast_block_rows(n_rows, n_steps):
    rows = V7X_BF16_SUBLANES
    while n_rows % rows or n_rows // rows > n_steps:
        rows += V7X_BF16_SUBLANES
    return rows


def _cast_specs(to_cast, n_steps, step_of):
    specs = []
    for w in to_cast:
        rows = _cast_block_rows(w.shape[0], n_steps)
        last = w.shape[0] // rows - 1
        specs.append(pl.BlockSpec(
            (rows, w.shape[1]), lambda *idx, last=last: (jnp.minimum(step_of(*idx), last), 0)))
    return specs


def _cast_blocks(srcs, dsts):
    for src, dst in zip(srcs, dsts):
        dst[...] = src[...].astype(BF16)


def _ffn1_kernel(x_ref, n1_ref, wg_hbm, wu_hbm, wd_hbm, n2_ref, win_hbm, h_ref, p_ref, s_ref,
                 wg_ref, wu_ref, wd_ref, stage_up, stage_down, sem, wps_ref):
    @pl.when(_is_first_step_2d())
    def _():
        n_ps = D_POOL + D_SSM
        _load_weights_as_bf16([
            (wg_hbm, wg_ref, stage_up),
            (wu_hbm, wu_ref, stage_up),
            (wd_hbm, wd_ref, stage_down),
            (win_hbm.at[:, pl.ds(0, n_ps)], wps_ref, stage_up.at[:, :, pl.ds(0, n_ps)]),
        ], sem)

    def chain(rows):
        h = yield from _swiglu_residual(x_ref[rows, :], n1_ref[...], wg_ref, wu_ref, wd_ref)
        h_ref[rows, :] = h
        u = _rms(h, n2_ref[...]).astype(BF16)
        ps = _dot(u, wps_ref[...])
        p_ref[rows, :] = ps[:, :D_POOL]
        s_ref[rows, :] = ps[:, D_POOL:]

    _trace_round_robin([chain(rows) for rows in _row_groups(x_ref.shape[0])], skew=FFN_SKEW)


def _ffn1_call(x, n1, wg, wu, wd, n2, w_in):
    B, L, D = x.shape
    T = FFN_TILE
    tok = pl.BlockSpec((None, T, D), lambda b, i: (b, i, 0))
    return pl.pallas_call(
        _ffn1_kernel,
        grid=(B, L // T),
        in_specs=[tok, _const_spec((1, D)), _HBM, _HBM, _HBM, _const_spec((1, D)), _HBM],
        out_specs=[
            tok,
            pl.BlockSpec((None, T, D_POOL), lambda b, i: (b, i, 0)),
            pl.BlockSpec((None, T, D_SSM), lambda b, i: (b, i, 0)),
        ],
        out_shape=[
            jax.ShapeDtypeStruct((B, L, D), F32),
            jax.ShapeDtypeStruct((B, L, D_POOL), F32),
            jax.ShapeDtypeStruct((B, L, D_SSM), F32),
        ],
        scratch_shapes=_swiglu_weight_scratch(D) + [pltpu.VMEM((D, D_POOL + D_SSM), BF16)],
        compiler_params=pltpu.CompilerParams(
            dimension_semantics=("arbitrary", "arbitrary"), vmem_limit_bytes=V7X_VMEM_LIMIT_BYTES),
        name="ffn1",
    )(x, n1, wg, wu, wd, n2, w_in)


def _ffn2_kernel(h_ref, n_ref, wg_ref, wu_ref, wd_ref, fn_ref, o_ref):
    def chain(rows):
        h = yield from _swiglu_residual(h_ref[rows, :], n_ref[...], wg_ref, wu_ref, wd_ref)
        o_ref[rows, :] = _rms(h, fn_ref[...])

    _trace_round_robin([chain(rows) for rows in _row_groups(h_ref.shape[0])], skew=FFN_SKEW)


def _ffn2_call(h, n, wg, wu, wd, fn):
    B, L, D = h.shape
    T = FFN_TILE
    tok = pl.BlockSpec((None, T, D), lambda b, i: (b, i, 0))
    return pl.pallas_call(
        _ffn2_kernel,
        grid=(B, L // T),
        in_specs=[tok, _const_spec((1, D)), _const_spec((D, D_FF)), _const_spec((D, D_FF)),
                  _const_spec((D_FF, D)), _const_spec((1, D))],
        out_specs=tok,
        out_shape=jax.ShapeDtypeStruct((B, L, D), F32),
        compiler_params=pltpu.CompilerParams(
            dimension_semantics=("arbitrary", "arbitrary"), vmem_limit_bytes=V7X_VMEM_LIMIT_BYTES),
        name="ffn2",
    )(h, n, wg, wu, wd, fn)


def _ssm_kernel(*refs, n_cast):
    (sf_ref, sb_ref, are_ref, aim_ref, ldt_ref, bre_ref, bim_ref, crt_ref, cit_ref, d_ref), refs = refs[:10], refs[10:]
    cast_src, refs = refs[:n_cast], refs[n_cast:]
    (yf_ref, yb_ref), refs = refs[:2], refs[2:]
    cast_dst, refs = refs[:n_cast], refs[n_cast:]
    abar_scr, bmat_scr, cmat_scr, state_scr, bu_scr, xb_scr = refs
    i = pl.program_id(0)
    S = D_STATE
    n_b, steps, _ = sf_ref.shape
    R = steps * n_b

    @pl.when(i == 0)
    def _():
        for d in range(2):
            ar = are_ref[d]
            ai = aim_ref[d]
            dt = jnp.exp(ldt_ref[d])
            mag = jnp.exp(dt * ar)
            ang = dt * ai
            abr = mag * jnp.cos(ang)
            abi = mag * jnp.sin(ang)
            den = ar * ar + ai * ai
            nr = abr - 1.0
            qr = (nr * ar + abi * ai) / den
            qi = (abi * ar - nr * ai) / den
            abar_scr[d, 0] = jnp.broadcast_to(abr, (8, S))
            abar_scr[d, 1] = jnp.broadcast_to(abi, (8, S))
            br = bre_ref[d]
            bi = bim_ref[d]
            bmat_scr[d, :, :S] = (qr * br - qi * bi).astype(BF16)
            bmat_scr[d, :, S:] = (qr * bi + qi * br).astype(BF16)
            cmat_scr[d, :S, :] = crt_ref[d]
            cmat_scr[d, S:, :] = -cit_ref[d]
        state_scr[...] = jnp.zeros_like(state_scr)

    sub = SSM_SUBTILE
    n_sub = steps // sub
    halves = [(slice(c0, c0 + SCAN_LANES), slice(S + c0, S + c0 + SCAN_LANES))
              for c0 in range(0, S, SCAN_LANES)]
    def chain(d, s_ref, y_ref):
        order = list(range(n_sub)) if d == 0 else list(range(n_sub - 1, -1, -1))

        def input_map(j):
            s_nat = s_ref[:, j * sub:(j + 1) * sub, :]
            s_tb = jnp.swapaxes(s_nat, 0, 1).reshape(sub * n_b, D_SSM)
            bu_scr[d, j * sub * n_b:(j + 1) * sub * n_b, :] = _dot(s_tb.astype(BF16), bmat_scr[d])

        coef = [(abar_scr[d, 0, :, re], abar_scr[d, 1, :, re]) for re, _ in halves]
        state = [(state_scr[d, 0, :, re], state_scr[d, 1, :, re]) for re, _ in halves]
        for j in order[:SSM_AHEAD]:
            input_map(j)
        yield
        for pos, j in enumerate(order):
            if pos + SSM_AHEAD < n_sub:
                input_map(order[pos + SSM_AHEAD])
            for hf, (re, im) in enumerate(halves):
                ar, ai = coef[hf]
                xr, xi = state[hf]
                for k in range(0, sub, 2):
                    lo = j * sub + (k if d == 0 else sub - 2 - k)
                    got = {}
                    for t in ((lo, lo + 1) if d == 0 else (lo + 1, lo)):
                        rows = slice(n_b * t, n_b * (t + 1))
                        nxr = ar * xr - ai * xi + bu_scr[d, rows, re]
                        nxi = ar * xi + ai * xr + bu_scr[d, rows, im]
                        xr, xi = nxr, nxi
                        got[t] = (nxr, nxi)
                    rows2 = slice(n_b * lo, n_b * (lo + 2))
                    xb_scr[d, rows2, re] = jnp.concatenate([got[lo][0], got[lo + 1][0]], axis=0).astype(BF16)
                    xb_scr[d, rows2, im] = jnp.concatenate([got[lo][1], got[lo + 1][1]], axis=0).astype(BF16)
                state[hf] = (xr, xi)
            rows_j = slice(j * sub * n_b, (j + 1) * sub * n_b)
            y_tb = _dot(xb_scr[d, rows_j, :], cmat_scr[d])
            y = jnp.swapaxes(y_tb.reshape(sub, n_b, D_SSM), 0, 1)
            if d == 0:
                y = y + sf_ref[:, j * sub:(j + 1) * sub, :] * d_ref[...]
            y_ref[:, j * sub:(j + 1) * sub, :] = y
            yield
        for hf, (re, _) in enumerate(halves):
            state_scr[d, 0, :, re] = state[hf][0]
            state_scr[d, 1, :, re] = state[hf][1]

    _trace_round_robin([chain(0, sf_ref, yf_ref), chain(1, sb_ref, yb_ref)])
    _cast_blocks(cast_src, cast_dst)


def _ssm_call(s, a_re, a_im, log_dt, b_re_bd, b_im_bd, c_re_t, c_im_t, d_skip, to_cast):
    B, L, _ = s.shape
    assert B == V7X_SUBLANES, "the scan keeps one batch element per sublane"
    T = SSM_TILE
    n = L // T
    S = D_STATE
    fwd = pl.BlockSpec((B, T, D_SSM), lambda i: (0, i, 0))
    bwd = pl.BlockSpec((B, T, D_SSM), lambda i: (0, n - 1 - i, 0))
    cast_specs = _cast_specs(to_cast, n, lambda i: i)
    outs = pl.pallas_call(
        functools.partial(_ssm_kernel, n_cast=len(to_cast)),
        grid=(n,),
        in_specs=[
            fwd, bwd,
            _const_spec((2, 1, S)), _const_spec((2, 1, S)), _const_spec((2, 1, S)),
            _const_spec((2, D_SSM, S)), _const_spec((2, D_SSM, S)),
            _const_spec((2, S, D_SSM)), _const_spec((2, S, D_SSM)),
            _const_spec((1, D_SSM)),
        ] + cast_specs,
        out_specs=[fwd, bwd] + cast_specs,
        out_shape=[jax.ShapeDtypeStruct((B, L, D_SSM), F32)] * 2
                  + [jax.ShapeDtypeStruct(w.shape, BF16) for w in to_cast],
        scratch_shapes=[
            pltpu.VMEM((2, 2, 8, S), F32),
            pltpu.VMEM((2, D_SSM, 2 * S), BF16),
            pltpu.VMEM((2, 2 * S, D_SSM), BF16),
            pltpu.VMEM((2, 2, 8, S), F32),
            pltpu.VMEM((2, B * T, 2 * S), F32),
            pltpu.VMEM((2, B * T, 2 * S), BF16),
        ],
        compiler_params=pltpu.CompilerParams(
            dimension_semantics=("arbitrary",), vmem_limit_bytes=V7X_VMEM_LIMIT_BYTES),
        name="ssm_scan",
    )(s, s, a_re, a_im, log_dt, b_re_bd, b_im_bd, c_re_t, c_im_t, d_skip, *to_cast)
    return outs[0], outs[1], outs[2:]


def _window_sum(v, w):
    n = v.shape[0]
    ahead = lambda a, k: pltpu.roll(a, n - k, axis=0)
    behind = lambda a, k: pltpu.roll(a, k, axis=0)
    if w == 2:
        return v + behind(v, 1)
    cur, span = v + ahead(v, 1), 2
    while span < w // 2:
        cur, span = cur + ahead(cur, span), 2 * span
    return cur + behind(cur, span)


def _mix_kernel(h_ref, p_ref, pl_ref, pr_ref, yf_ref, yb_ref, k_ref, v_ref,
                nmix_ref, win_ref, poolw_ref, pscale_ref, wpp_ref, wgluv_ref, wglug_ref, wmo_ref,
                nx_ref, wq_ref, wxo_ref, o_ref, *, seq_len):
    i = pl.program_id(1)
    n_i = pl.num_programs(1)
    T = h_ref.shape[0]
    H = POOL_HALO

    halo_l = jnp.where(i > 0, pl_ref[...], 0.0)
    halo_r = jnp.where(i < n_i - 1, pr_ref[...], 0.0)
    edge = V7X_SUBLANES
    t_edge = lax.broadcasted_iota(jnp.int32, (edge, POOL_GROUP), 0)
    pooled_groups = []
    for g, w in enumerate(POOL_WINDOWS):
        left = w // 2
        right = w - 1 - left
        lanes = slice(g * POOL_GROUP, (g + 1) * POOL_GROUP)
        v = jnp.concatenate([halo_l[:, lanes], p_ref[:, lanes], halo_r[:, lanes]], axis=0)
        tot = _window_sum(v, w)[H:H + T]
        self_ = v[H:H + T]

        def clipped(t0, rows):
            t = i * T + t0 + t_edge
            cnt = jnp.minimum(t + (right + 1), seq_len) - jnp.maximum(t - left, 0)
            return tot[rows] / cnt.astype(F32) - self_[rows]

        pooled_groups.append(jnp.concatenate([
            clipped(0, slice(0, edge)),
            tot[edge:T - edge] * (1.0 / w) - self_[edge:T - edge],
            clipped(T - edge, slice(T - edge, T)),
        ], axis=0).astype(BF16))

    def chain(rows):
        h = h_ref[rows, :]
        mixed = [_dot(pooled_groups[g][rows], poolw_ref[g]) for g in range(len(POOL_WINDOWS))]
        yield
        mixed = jnp.concatenate(mixed, axis=-1) * pscale_ref[...]
        z_pool = _dot(mixed.astype(BF16), wpp_ref[...])
        yield

        y = jax.nn.gelu(yf_ref[rows, :] + yb_ref[rows, :]).astype(BF16)
        glu_val = _dot(y, wgluv_ref[...])
        glu_gate = _dot(y, wglug_ref[...])
        yield
        z_ssm = glu_val * jax.nn.sigmoid(glu_gate)

        u = _rms(h, nmix_ref[...]).astype(BF16)
        gates = _dot(u, win_ref[:, D_POOL + D_SSM:])
        yield
        merged = (jax.nn.sigmoid(gates[:, :D_MODEL]) * z_pool
                  + jax.nn.sigmoid(gates[:, D_MODEL:]) * z_ssm)
        h = h + _dot(merged.astype(BF16), wmo_ref[...])
        yield

        q = _dot(_rms(h, nx_ref[...]).astype(BF16), wq_ref[...])
        yield
        head_cols = [slice(hd * XHEAD_DIM, (hd + 1) * XHEAD_DIM) for hd in range(N_XHEADS)]
        scores = [lax.dot_general(q[:, cols].astype(BF16), k_ref[:, cols], (((1,), (1,)), ((), ())),
                                  preferred_element_type=F32) / math.sqrt(XHEAD_DIM) for cols in head_cols]
        yield
        heads = []
        for sc, cols in zip(scores, head_cols):
            e = jnp.exp(sc - jnp.max(sc, axis=-1, keepdims=True))
            probs = e / jnp.sum(e, axis=-1, keepdims=True)
            heads.append(_dot(probs.astype(BF16), v_ref[:, cols]))
        yield
        o = jnp.concatenate(heads, axis=-1).astype(BF16)
        o_ref[rows, :] = h + _dot(o, wxo_ref[...])

    _trace_round_robin([chain(rows) for rows in _row_groups(T)], skew=MIX_SKEW)


def _mix_call(h, p, yf, yb, k, v, nmix, wgate, poolw, pscale, wpp, wgluv, wglug, wmo, nx, wq, wxo):
    B, L, D = h.shape
    T = MIX_TILE
    H = POOL_HALO
    per_tile = T // H
    n_halo = L // H
    tok = pl.BlockSpec((None, T, D), lambda b, i: (b, i, 0))
    y_spec = pl.BlockSpec((None, T, D_SSM), lambda b, i: (b, i, 0))
    mem_spec = pl.BlockSpec((None, N_MEM, D), lambda b, i: (b, 0, 0))
    return pl.pallas_call(
        functools.partial(_mix_kernel, seq_len=L),
        grid=(B, L // T),
        in_specs=[
            tok,
            pl.BlockSpec((None, T, D_POOL), lambda b, i: (b, i, 0)),
            pl.BlockSpec((None, H, D_POOL), lambda b, i: (b, jnp.maximum(i * per_tile - 1, 0), 0)),
            pl.BlockSpec((None, H, D_POOL), lambda b, i: (b, jnp.minimum((i + 1) * per_tile, n_halo - 1), 0)),
            y_spec, y_spec,
            mem_spec, mem_spec,
            _const_spec((1, D)),
            _const_spec((D, D_POOL + D_SSM + 2 * D)),
            _const_spec((len(POOL_WINDOWS), POOL_GROUP, POOL_GROUP)),
            _const_spec((1, D_POOL)),
            _const_spec((D_POOL, D)),
            _const_spec((D_SSM, D)),
            _const_spec((D_SSM, D)),
            _const_spec((D, D)),
            _const_spec((1, D)),
            _const_spec((D, D)),
            _const_spec((D, D)),
        ],
        out_specs=tok,
        out_shape=jax.ShapeDtypeStruct((B, L, D), F32),
        compiler_params=pltpu.CompilerParams(
            dimension_semantics=("arbitrary", "arbitrary"), vmem_limit_bytes=V7X_VMEM_LIMIT_BYTES),
        name="mix_attn",
    )(h, p, p, p, yf, yb, k, v, nmix, wgate, poolw, pscale, wpp, wgluv, wglug, wmo, nx, wq, wxo)


def _block_diag_in(w):
    G, P, Hh = w.shape
    eye = jnp.eye(G, dtype=w.dtype)
    return jnp.einsum("gph,gk->ghkp", w, eye).reshape(G * Hh, G * P)


def _block_diag_out(w):
    G, Hh, P = w.shape
    eye = jnp.eye(G, dtype=w.dtype)
    return jnp.einsum("ghp,gk->gpkh", w, eye).reshape(G * P, G * Hh)


def kernel(x, mem, ffn1_norm, ffn1_w_gate, ffn1_w_up, ffn1_w_down, mix_norm, w_in, pool_w, pool_scale,
           w_pool_proj, ssm_a_re, ssm_a_im, ssm_log_dt, ssm_b_re, ssm_b_im, ssm_c_re, ssm_c_im, ssm_d,
           w_glu_val, w_glu_gate, w_mix_out, xattn_norm, mem_norm, w_q, w_kv, w_xo, ffn2_norm,
           ffn2_w_gate, ffn2_w_up, ffn2_w_down, final_norm):
    B, L, D = x.shape
    depth = ffn1_norm.shape[0]
    row = lambda v: v.reshape(1, -1)
    bf = lambda w: w.astype(BF16)
    h = x
    for l in range(depth):
        k_mem, v_mem, cast = _kv_call(
            mem, row(mem_norm[l]), w_kv[l],
            [w_in[l], pool_w[l].reshape(-1, POOL_GROUP), w_pool_proj[l], w_glu_val[l], w_glu_gate[l],
             w_mix_out[l], w_q[l], w_xo[l]])
        w_in_b, pool_w_b, w_pp_b, w_gluv_b, w_glug_b, w_mo_b, w_q_b, w_xo_b = cast

        h, p, s = _ffn1_call(
            h, row(ffn1_norm[l]), ffn1_w_gate[l], ffn1_w_up[l], ffn1_w_down[l], row(mix_norm[l]), w_in[l])

        G, P = N_SSM_GROUPS, SSM_STATE
        per_state = lambda a: a.reshape(2, 1, G * P)
        log_dt = jnp.broadcast_to(ssm_log_dt[l][:, :, None], (2, G, P))
        yf, yb, (ffn2_wg_b, ffn2_wu_b, ffn2_wd_b) = _ssm_call(
            s, per_state(ssm_a_re[l]), per_state(ssm_a_im[l]), per_state(log_dt),
            jax.vmap(_block_diag_in)(ssm_b_re[l]), jax.vmap(_block_diag_in)(ssm_b_im[l]),
            bf(jax.vmap(_block_diag_out)(ssm_c_re[l])), bf(jax.vmap(_block_diag_out)(ssm_c_im[l])),
            row(ssm_d[l]), [ffn2_w_gate[l], ffn2_w_up[l], ffn2_w_down[l]])

        h = _mix_call(
            h, p, yf, yb, k_mem, v_mem,
            row(mix_norm[l]), w_in_b, pool_w_b.reshape(pool_w[l].shape), row(pool_scale[l]),
            w_pp_b, w_gluv_b, w_glug_b, w_mo_b, row(xattn_norm[l]), w_q_b, w_xo_b)

        last = l == depth - 1
        assert last, "deeper stacks need a non-final variant of the second SwiGLU kernel"
        h = _ffn2_call(h, row(ffn2_norm[l]), ffn2_wg_b, ffn2_wu_b, ffn2_wd_b, row(final_norm))
    return h
```

```python
import functools
import math

import jax
import jax.numpy as jnp
from jax import lax
from jax.experimental import pallas as pl
from jax.experimental.pallas import tpu as pltpu

F32 = jnp.float32
BF16 = jnp.bfloat16

D_MODEL = 1024
N_MEM = 256
D_FF = 2816
D_POOL = 512
POOL_WINDOWS = (2, 4, 8, 16)
POOL_GROUP = 128
D_SSM = 256
SSM_GROUP = 16
N_SSM_GROUPS = 16
SSM_STATE = 64
D_STATE = N_SSM_GROUPS * SSM_STATE
N_XHEADS = 4
XHEAD_DIM = 256
EPS = 1e-6

V7X_VMEM_LIMIT_BYTES = 58 * 1024 * 1024
V7X_SUBLANES = 8
V7X_BF16_SUBLANES = 16
POOL_HALO = 16

FFN_TILE = 1024
MIX_TILE = 1024
SSM_TILE = 128
FFN_SKEW = 1
MIX_SKEW = 1
WEIGHT_CHUNKS = 16
WEIGHT_SLOTS = 4
GROUP_ROWS = 256
FF_CHUNKS = ((0, 768), (768, 1536), (1536, 2304), (2304, 2816))
SSM_AHEAD = 2
SSM_SUBTILE = 16
SCAN_LANES = 512


def _rms(v, g):
    r = lax.rsqrt(jnp.mean(v * v, axis=-1, keepdims=True) + EPS)
    return (v * r) * g


def _dot(a, b):
    return jnp.dot(a, b, preferred_element_type=F32)


def _const_spec(shape):
    nd = len(shape)
    return pl.BlockSpec(shape, lambda *_: (0,) * nd)


def _kv_kernel(*refs, n_cast):
    (mem_ref, g_ref, wkv_ref), refs = refs[:3], refs[3:]
    cast_src, refs = refs[:n_cast], refs[n_cast:]
    (k_ref, v_ref), refs = refs[:2], refs[2:]
    cast_dst, (wkv_bf16,) = refs[:n_cast], refs[n_cast:]

    @pl.when(pl.program_id(0) == 0)
    def _():
        wkv_bf16[...] = wkv_ref[...].astype(BF16)

    m = _rms(mem_ref[...], g_ref[...]).astype(BF16)
    kv = _dot(m, wkv_bf16[...])
    k_ref[...] = kv[:, :D_MODEL].astype(BF16)
    v_ref[...] = kv[:, D_MODEL:].astype(BF16)
    _cast_blocks(cast_src, cast_dst)


def _kv_call(mem, mem_norm, w_kv, to_cast):
    B = mem.shape[0]
    per_batch = pl.BlockSpec((None, N_MEM, D_MODEL), lambda b: (b, 0, 0))
    cast_specs = _cast_specs(to_cast, B, lambda b: b)
    outs = pl.pallas_call(
        functools.partial(_kv_kernel, n_cast=len(to_cast)),
        grid=(B,),
        in_specs=[per_batch, _const_spec((1, D_MODEL)), _const_spec((D_MODEL, 2 * D_MODEL))] + cast_specs,
        out_specs=[per_batch, per_batch] + cast_specs,
        out_shape=[jax.ShapeDtypeStruct((B, N_MEM, D_MODEL), BF16)] * 2
                  + [jax.ShapeDtypeStruct(w.shape, BF16) for w in to_cast],
        scratch_shapes=[pltpu.VMEM((D_MODEL, 2 * D_MODEL), BF16)],
        compiler_params=pltpu.CompilerParams(
            dimension_semantics=("arbitrary",), vmem_limit_bytes=V7X_VMEM_LIMIT_BYTES),
        name="kv_proj",
    )(mem, mem_norm, w_kv, *to_cast)
    return outs[0], outs[1], outs[2:]


def _row_groups(n_rows):
    return [slice(r, r + GROUP_ROWS) for r in range(0, n_rows, GROUP_ROWS)]


def _trace_round_robin(chains, skew=0):
    waiting = list(chains)
    n_started = 0
    live = []
    turn = 0
    while waiting or live:
        while waiting and turn >= n_started * skew:
            live.append(waiting.pop(0))
            n_started += 1
        for c in list(live):
            try:
                next(c)
            except StopIteration:
                live.remove(c)
        turn += 1


def _load_weights_as_bf16(jobs, sem):
    chunks = []
    for src, dst, stage in jobs:
        rows = stage.shape[1]
        for r0 in range(0, src.shape[0], rows):
            chunks.append((src.at[pl.ds(r0, rows)], dst.at[pl.ds(r0, rows)], stage))
    ahead = WEIGHT_SLOTS

    def copy(k):
        src, _, stage = chunks[k]
        return pltpu.make_async_copy(src, stage.at[k % WEIGHT_SLOTS], sem.at[k % WEIGHT_SLOTS])

    for k in range(min(ahead, len(chunks))):
        copy(k).start()
    for k, (_, dst, stage) in enumerate(chunks):
        copy(k).wait()
        dst[...] = stage[k % WEIGHT_SLOTS].astype(BF16)
        if k + ahead < len(chunks):
            copy(k + ahead).start()


def _is_first_step_2d():
    return jnp.logical_and(pl.program_id(0) == 0, pl.program_id(1) == 0)


def _swiglu_residual(x, g, wg_ref, wu_ref, wd_ref):
    u = _rms(x, g).astype(BF16)
    acc = None
    for c0, c1 in FF_CHUNKS:
        gate = _dot(u, wg_ref[:, c0:c1])
        up = _dot(u, wu_ref[:, c0:c1])
        yield
        act = (gate * jax.nn.sigmoid(gate) * up).astype(BF16)
        part = _dot(act, wd_ref[c0:c1, :])
        acc = part if acc is None else acc + part
        yield
    return x + 0.5 * acc


_HBM = pl.BlockSpec(memory_space=pl.ANY)


def _swiglu_weight_scratch(D):
    return [
        pltpu.VMEM((D, D_FF), BF16),
        pltpu.VMEM((D, D_FF), BF16),
        pltpu.VMEM((D_FF, D), BF16),
        pltpu.VMEM((WEIGHT_SLOTS, _cast_block_rows(D, WEIGHT_CHUNKS), D_FF), F32),
        pltpu.VMEM((WEIGHT_SLOTS, _cast_block_rows(D_FF, WEIGHT_CHUNKS), D), F32),
        pltpu.SemaphoreType.DMA((WEIGHT_SLOTS,)),
    ]


def _cast_block_rows(n_rows, n_steps):
    rows = V7X_BF16_SUBLANES
    while n_rows % rows or n_rows // rows > n_steps:
        rows += V7X_BF16_SUBLANES
    return rows


def _cast_specs(to_cast, n_steps, step_of):
    specs = []
    for w in to_cast:
        rows = _cast_block_rows(w.shape[0], n_steps)
        last = w.shape[0] // rows - 1
        specs.append(pl.BlockSpec(
            (rows, w.shape[1]), lambda *idx, last=last: (jnp.minimum(step_of(*idx), last), 0)))
    return specs


def _cast_blocks(srcs, dsts):
    for src, dst in zip(srcs, dsts):
        dst[...] = src[...].astype(BF16)


def _ffn1_kernel(x_ref, n1_ref, wg_hbm, wu_hbm, wd_hbm, n2_ref, win_hbm, h_ref, p_ref, s_ref,
                 wg_ref, wu_ref, wd_ref, stage_up, stage_down, sem, wps_ref):
    @pl.when(_is_first_step_2d())
    def _():
        n_ps = D_POOL + D_SSM
        _load_weights_as_bf16([
            (wg_hbm, wg_ref, stage_up),
            (wu_hbm, wu_ref, stage_up),
            (wd_hbm, wd_ref, stage_down),
            (win_hbm.at[:, pl.ds(0, n_ps)], wps_ref, stage_up.at[:, :, pl.ds(0, n_ps)]),
        ], sem)

    def chain(rows):
        h = yield from _swiglu_residual(x_ref[rows, :], n1_ref[...], wg_ref, wu_ref, wd_ref)
        h_ref[rows, :] = h
        u = _rms(h, n2_ref[...]).astype(BF16)
        ps = _dot(u, wps_ref[...])
        p_ref[rows, :] = ps[:, :D_POOL]
        s_ref[rows, :] = ps[:, D_POOL:]

    _trace_round_robin([chain(rows) for rows in _row_groups(x_ref.shape[0])], skew=FFN_SKEW)


def _ffn1_call(x, n1, wg, wu, wd, n2, w_in):
    B, L, D = x.shape
    T = FFN_TILE
    tok = pl.BlockSpec((None, T, D), lambda b, i: (b, i, 0))
    return pl.pallas_call(
        _ffn1_kernel,
        grid=(B, L // T),
        in_specs=[tok, _const_spec((1, D)), _HBM, _HBM, _HBM, _const_spec((1, D)), _HBM],
        out_specs=[
            tok,
            pl.BlockSpec((None, T, D_POOL), lambda b, i: (b, i, 0)),
            pl.BlockSpec((None, T, D_SSM), lambda b, i: (b, i, 0)),
        ],
        out_shape=[
            jax.ShapeDtypeStruct((B, L, D), F32),
            jax.ShapeDtypeStruct((B, L, D_POOL), F32),
            jax.ShapeDtypeStruct((B, L, D_SSM), F32),
        ],
        scratch_shapes=_swiglu_weight_scratch(D) + [pltpu.VMEM((D, D_POOL + D_SSM), BF16)],
        compiler_params=pltpu.CompilerParams(
            dimension_semantics=("arbitrary", "arbitrary"), vmem_limit_bytes=V7X_VMEM_LIMIT_BYTES),
        name="ffn1",
    )(x, n1, wg, wu, wd, n2, w_in)


def _ffn2_kernel(h_ref, n_ref, wg_ref, wu_ref, wd_ref, fn_ref, o_ref):
    def chain(rows):
        h = yield from _swiglu_residual(h_ref[rows, :], n_ref[...], wg_ref, wu_ref, wd_ref)
        o_ref[rows, :] = _rms(h, fn_ref[...])

    _trace_round_robin([chain(rows) for rows in _row_groups(h_ref.shape[0])], skew=FFN_SKEW)


def _ffn2_call(h, n, wg, wu, wd, fn):
    B, L, D = h.shape
    T = FFN_TILE
    tok = pl.BlockSpec((None, T, D), lambda b, i: (b, i, 0))
    return pl.pallas_call(
        _ffn2_kernel,
        grid=(B, L // T),
        in_specs=[tok, _const_spec((1, D)), _const_spec((D, D_FF)), _const_spec((D, D_FF)),
                  _const_spec((D_FF, D)), _const_spec((1, D))],
        out_specs=tok,
        out_shape=jax.ShapeDtypeStruct((B, L, D), F32),
        compiler_params=pltpu.CompilerParams(
            dimension_semantics=("arbitrary", "arbitrary"), vmem_limit_bytes=V7X_VMEM_LIMIT_BYTES),
        name="ffn2",
    )(h, n, wg, wu, wd, fn)


def _ssm_kernel(*refs, n_cast):
    (sf_ref, sb_ref, are_ref, aim_ref, ldt_ref, bre_ref, bim_ref, crt_ref, cit_ref, d_ref), refs = refs[:10], refs[10:]
    cast_src, refs = refs[:n_cast], refs[n_cast:]
    (yf_ref, yb_ref), refs = refs[:2], refs[2:]
    cast_dst, refs = refs[:n_cast], refs[n_cast:]
    abar_scr, bmat_scr, cmat_scr, state_scr, bu_scr, xb_scr = refs
    i = pl.program_id(0)
    S = D_STATE
    n_b, steps, _ = sf_ref.shape
    R = steps * n_b

    @pl.when(i == 0)
    def _():
        for d in range(2):
            ar = are_ref[d]
            ai = aim_ref[d]
            dt = jnp.exp(ldt_ref[d])
            mag = jnp.exp(dt * ar)
            ang = dt * ai
            abr = mag * jnp.cos(ang)
            abi = mag * jnp.sin(ang)
            den = ar * ar + ai * ai
            nr = abr - 1.0
            qr = (nr * ar + abi * ai) / den
            qi = (abi * ar - nr * ai) / den
            abar_scr[d, 0] = jnp.broadcast_to(abr, (8, S))
            abar_scr[d, 1] = jnp.broadcast_to(abi, (8, S))
            br = bre_ref[d]
            bi = bim_ref[d]
            bmat_scr[d, :, :S] = (qr * br - qi * bi).astype(BF16)
            bmat_scr[d, :, S:] = (qr * bi + qi * br).astype(BF16)
            cmat_scr[d, :S, :] = crt_ref[d]
            cmat_scr[d, S:, :] = -cit_ref[d]
        state_scr[...] = jnp.zeros_like(state_scr)

    sub = SSM_SUBTILE
    n_sub = steps // sub
    halves = [(slice(c0, c0 + SCAN_LANES), slice(S + c0, S + c0 + SCAN_LANES))
              for c0 in range(0, S, SCAN_LANES)]
    def chain(d, s_ref, y_ref):
        order = list(range(n_sub)) if d == 0 else list(range(n_sub - 1, -1, -1))

        def input_map(j):
            s_nat = s_ref[:, j * sub:(j + 1) * sub, :]
            s_tb = jnp.swapaxes(s_nat, 0, 1).reshape(sub * n_b, D_SSM)
            bu_scr[d, j * sub * n_b:(j + 1) * sub * n_b, :] = _dot(s_tb.astype(BF16), bmat_scr[d])

        coef = [(abar_scr[d, 0, :, re], abar_scr[d, 1, :, re]) for re, _ in halves]
        state = [(state_scr[d, 0, :, re], state_scr[d, 1, :, re]) for re, _ in halves]
        for j in order[:SSM_AHEAD]:
            input_map(j)
        yield
        for pos, j in enumerate(order):
            if pos + SSM_AHEAD < n_sub:
                input_map(order[pos + SSM_AHEAD])
            for hf, (re, im) in enumerate(halves):
                ar, ai = coef[hf]
                xr, xi = state[hf]
                for k in range(0, sub, 2):
                    lo = j * sub + (k if d == 0 else sub - 2 - k)
                    got = {}
                    for t in ((lo, lo + 1) if d == 0 else (lo + 1, lo)):
                        rows = slice(n_b * t, n_b * (t + 1))
                        nxr = ar * xr - ai * xi + bu_scr[d, rows, re]
                        nxi = ar * xi + ai * xr + bu_scr[d, rows, im]
                        xr, xi = nxr, nxi
                        got[t] = (nxr, nxi)
                    rows2 = slice(n_b * lo, n_b * (lo + 2))
                    xb_scr[d, rows2, re] = jnp.concatenate([got[lo][0], got[lo + 1][0]], axis=0).astype(BF16)
                    xb_scr[d, rows2, im] = jnp.concatenate([got[lo][1], got[lo + 1][1]], axis=0).astype(BF16)
                state[hf] = (xr, xi)
            rows_j = slice(j * sub * n_b, (j + 1) * sub * n_b)
            y_tb = _dot(xb_scr[d, rows_j, :], cmat_scr[d])
            y = jnp.swapaxes(y_tb.reshape(sub, n_b, D_SSM), 0, 1)
            if d == 0:
                y = y + sf_ref[:, j * sub:(j + 1) * sub, :] * d_ref[...]
            y_ref[:, j * sub:(j + 1) * sub, :] = y
            yield
        for hf, (re, _) in enumerate(halves):
            state_scr[d, 0, :, re] = state[hf][0]
            state_scr[d, 1, :, re] = state[hf][1]

    _trace_round_robin([chain(0, sf_ref, yf_ref), chain(1, sb_ref, yb_ref)])
    _cast_blocks(cast_src, cast_dst)


def _ssm_call(s, a_re, a_im, log_dt, b_re_bd, b_im_bd, c_re_t, c_im_t, d_skip, to_cast):
    B, L, _ = s.shape
    assert B == V7X_SUBLANES, "the scan keeps one batch element per sublane"
    T = SSM_TILE
    n = L // T
    S = D_STATE
    fwd = pl.BlockSpec((B, T, D_SSM), lambda i: (0, i, 0))
    bwd = pl.BlockSpec((B, T, D_SSM), lambda i: (0, n - 1 - i, 0))
    cast_specs = _cast_specs(to_cast, n, lambda i: i)
    outs = pl.pallas_call(
        functools.partial(_ssm_kernel, n_cast=len(to_cast)),
        grid=(n,),
        in_specs=[
            fwd, bwd,
            _const_spec((2, 1, S)), _const_spec((2, 1, S)), _const_spec((2, 1, S)),
            _const_spec((2, D_SSM, S)), _const_spec((2, D_SSM, S)),
            _const_spec((2, S, D_SSM)), _const_spec((2, S, D_SSM)),
            _const_spec((1, D_SSM)),
        ] + cast_specs,
        out_specs=[fwd, bwd] + cast_specs,
        out_shape=[jax.ShapeDtypeStruct((B, L, D_SSM), F32)] * 2
                  + [jax.ShapeDtypeStruct(w.shape, BF16) for w in to_cast],
        scratch_shapes=[
            pltpu.VMEM((2, 2, 8, S), F32),
            pltpu.VMEM((2, D_SSM, 2 * S), BF16),
            pltpu.VMEM((2, 2 * S, D_SSM), BF16),
            pltpu.VMEM((2, 2, 8, S), F32),
            pltpu.VMEM((2, B * T, 2 * S), F32),
            pltpu.VMEM((2, B * T, 2 * S), BF16),
        ],
        compiler_params=pltpu.CompilerParams(
            dimension_semantics=("arbitrary",), vmem_limit_bytes=V7X_VMEM_LIMIT_BYTES),
        name="ssm_scan",
    )(s, s, a_re, a_im, log_dt, b_re_bd, b_im_bd, c_re_t, c_im_t, d_skip, *to_cast)
    return outs[0], outs[1], outs[2:]


def _window_sum(v, w):
    n = v.shape[0]
    ahead = lambda a, k: pltpu.roll(a, n - k, axis=0)
    behind = lambda a, k: pltpu.roll(a, k, axis=0)
    if w == 2:
        return v + behind(v, 1)
    cur, span = v + ahead(v, 1), 2
    while span < w // 2:
        cur, span = cur + ahead(cur, span), 2 * span
    return cur + behind(cur, span)


def _mix_kernel(h_ref, p_ref, pl_ref, pr_ref, yf_ref, yb_ref, k_ref, v_ref,
                nmix_ref, win_ref, poolw_ref, pscale_ref, wpp_ref, wgluv_ref, wglug_ref, wmo_ref,
                nx_ref, wq_ref, wxo_ref, o_ref, *, seq_len):
    i = pl.program_id(1)
    n_i = pl.num_programs(1)
    T = h_ref.shape[0]
    H = POOL_HALO

    halo_l = jnp.where(i > 0, pl_ref[...], 0.0)
    halo_r = jnp.where(i < n_i - 1, pr_ref[...], 0.0)
    edge = V7X_SUBLANES
    t_edge = lax.broadcasted_iota(jnp.int32, (edge, POOL_GROUP), 0)
    pooled_groups = []
    for g, w in enumerate(POOL_WINDOWS):
        left = w // 2
        right = w - 1 - left
        lanes = slice(g * POOL_GROUP, (g + 1) * POOL_GROUP)
        v = jnp.concatenate([halo_l[:, lanes], p_ref[:, lanes], halo_r[:, lanes]], axis=0)
        tot = _window_sum(v, w)[H:H + T]
        self_ = v[H:H + T]

        def clipped(t0, rows):
            t = i * T + t0 + t_edge
            cnt = jnp.minimum(t + (right + 1), seq_len) - jnp.maximum(t - left, 0)
            return tot[rows] / cnt.astype(F32) - self_[rows]

        pooled_groups.append(jnp.concatenate([
            clipped(0, slice(0, edge)),
            tot[edge:T - edge] * (1.0 / w) - self_[edge:T - edge],
            clipped(T - edge, slice(T - edge, T)),
        ], axis=0).astype(BF16))

    def chain(rows):
        h = h_ref[rows, :]
        mixed = [_dot(pooled_groups[g][rows], poolw_ref[g]) for g in range(len(POOL_WINDOWS))]
        yield
        mixed = jnp.concatenate(mixed, axis=-1) * pscale_ref[...]
        z_pool = _dot(mixed.astype(BF16), wpp_ref[...])
        yield

        y = jax.nn.gelu(yf_ref[rows, :] + yb_ref[rows, :]).astype(BF16)
        glu_val = _dot(y, wgluv_ref[...])
        glu_gate = _dot(y, wglug_ref[...])
        yield
        z_ssm = glu_val * jax.nn.sigmoid(glu_gate)

        u = _rms(h, nmix_ref[...]).astype(BF16)
        gates = _dot(u, win_ref[:, D_POOL + D_SSM:])
        yield
        merged = (jax.nn.sigmoid(gates[:, :D_MODEL]) * z_pool
                  + jax.nn.sigmoid(gates[:, D_MODEL:]) * z_ssm)
        h = h + _dot(merged.astype(BF16), wmo_ref[...])
        yield

        q = _dot(_rms(h, nx_ref[...]).astype(BF16), wq_ref[...])
        yield
        head_cols = [slice(hd * XHEAD_DIM, (hd + 1) * XHEAD_DIM) for hd in range(N_XHEADS)]
        scores = [lax.dot_general(q[:, cols].astype(BF16), k_ref[:, cols], (((1,), (1,)), ((), ())),
                                  preferred_element_type=F32) / math.sqrt(XHEAD_DIM) for cols in head_cols]
        yield
        heads = []
        for sc, cols in zip(scores, head_cols):
            e = jnp.exp(sc - jnp.max(sc, axis=-1, keepdims=True))
            probs = e / jnp.sum(e, axis=-1, keepdims=True)
            heads.append(_dot(probs.astype(BF16), v_ref[:, cols]))
        yield
        o = jnp.concatenate(heads, axis=-1).astype(BF16)
        o_ref[rows, :] = h + _dot(o, wxo_ref[...])

    _trace_round_robin([chain(rows) for rows in _row_groups(T)], skew=MIX_SKEW)


def _mix_call(h, p, yf, yb, k, v, nmix, wgate, poolw, pscale, wpp, wgluv, wglug, wmo, nx, wq, wxo):
    B, L, D = h.shape
    T = MIX_TILE
    H = POOL_HALO
    per_tile = T // H
    n_halo = L // H
    tok = pl.BlockSpec((None, T, D), lambda b, i: (b, i, 0))
    y_spec = pl.BlockSpec((None, T, D_SSM), lambda b, i: (b, i, 0))
    mem_spec = pl.BlockSpec((None, N_MEM, D), lambda b, i: (b, 0, 0))
    return pl.pallas_call(
        functools.partial(_mix_kernel, seq_len=L),
        grid=(B, L // T),
        in_specs=[
            tok,
            pl.BlockSpec((None, T, D_POOL), lambda b, i: (b, i, 0)),
            pl.BlockSpec((None, H, D_POOL), lambda b, i: (b, jnp.maximum(i * per_tile - 1, 0), 0)),
            pl.BlockSpec((None, H, D_POOL), lambda b, i: (b, jnp.minimum((i + 1) * per_tile, n_halo - 1), 0)),
            y_spec, y_spec,
            mem_spec, mem_spec,
            _const_spec((1, D)),
            _const_spec((D, D_POOL + D_SSM + 2 * D)),
            _const_spec((len(POOL_WINDOWS), POOL_GROUP, POOL_GROUP)),
            _const_spec((1, D_POOL)),
            _const_spec((D_POOL, D)),
            _const_spec((D_SSM, D)),
            _const_spec((D_SSM, D)),
            _const_spec((D, D)),
            _const_spec((1, D)),
            _const_spec((D, D)),
            _const_spec((D, D)),
        ],
        out_specs=tok,
        out_shape=jax.ShapeDtypeStruct((B, L, D), F32),
        compiler_params=pltpu.CompilerParams(
            dimension_semantics=("arbitrary", "arbitrary"), vmem_limit_bytes=V7X_VMEM_LIMIT_BYTES),
        name="mix_attn",
    )(h, p, p, p, yf, yb, k, v, nmix, wgate, poolw, pscale, wpp, wgluv, wglug, wmo, nx, wq, wxo)


def _block_diag_in(w):
    G, P, Hh = w.shape
    per_channel = jnp.transpose(w, (2, 0, 1)).reshape(Hh, G * P)
    same_group = (jnp.arange(G * Hh)[:, None] // Hh) == (jnp.arange(G * P)[None, :] // P)
    return jnp.where(same_group, jnp.tile(per_channel, (G, 1)), 0.0)


def _block_diag_out(w):
    G, Hh, P = w.shape
    per_state = jnp.transpose(w, (0, 2, 1)).reshape(G * P, Hh)
    same_group = (jnp.arange(G * P)[:, None] // P) == (jnp.arange(G * Hh)[None, :] // Hh)
    return jnp.where(same_group, jnp.tile(per_state, (1, G)), 0.0)


def kernel(x, mem, ffn1_norm, ffn1_w_gate, ffn1_w_up, ffn1_w_down, mix_norm, w_in, pool_w, pool_scale,
           w_pool_proj, ssm_a_re, ssm_a_im, ssm_log_dt, ssm_b_re, ssm_b_im, ssm_c_re, ssm_c_im, ssm_d,
           w_glu_val, w_glu_gate, w_mix_out, xattn_norm, mem_norm, w_q, w_kv, w_xo, ffn2_norm,
           ffn2_w_gate, ffn2_w_up, ffn2_w_down, final_norm):
    B, L, D = x.shape
    depth = ffn1_norm.shape[0]
    row = lambda v: v.reshape(1, -1)
    bf = lambda w: w.astype(BF16)
    h = x
    for l in range(depth):
        k_mem, v_mem, cast = _kv_call(
            mem, row(mem_norm[l]), w_kv[l],
            [w_in[l], pool_w[l].reshape(-1, POOL_GROUP), w_pool_proj[l], w_glu_val[l], w_glu_gate[l],
             w_mix_out[l], w_q[l], w_xo[l]])
        w_in_b, pool_w_b, w_pp_b, w_gluv_b, w_glug_b, w_mo_b, w_q_b, w_xo_b = cast

        h, p, s = _ffn1_call(
            h, row(ffn1_norm[l]), ffn1_w_gate[l], ffn1_w_up[l], ffn1_w_down[l], row(mix_norm[l]), w_in[l])

        G, P = N_SSM_GROUPS, SSM_STATE
        per_state = lambda a: a.reshape(2, 1, G * P)
        log_dt = jnp.broadcast_to(ssm_log_dt[l][:, :, None], (2, G, P))
        yf, yb, (ffn2_wg_b, ffn2_wu_b, ffn2_wd_b) = _ssm_call(
            s, per_state(ssm_a_re[l]), per_state(ssm_a_im[l]), per_state(log_dt),
            jax.vmap(_block_diag_in)(ssm_b_re[l]), jax.vmap(_block_diag_in)(ssm_b_im[l]),
            bf(jax.vmap(_block_diag_out)(ssm_c_re[l])), bf(jax.vmap(_block_diag_out)(ssm_c_im[l])),
            row(ssm_d[l]), [ffn2_w_gate[l], ffn2_w_up[l], ffn2_w_down[l]])

        h = _mix_call(
            h, p, yf, yb, k_mem, v_mem,
            row(mix_norm[l]), w_in_b, pool_w_b.reshape(pool_w[l].shape), row(pool_scale[l]),
            w_pp_b, w_gluv_b, w_glug_b, w_mo_b, row(xattn_norm[l]), w_q_b, w_xo_b)

        last = l == depth - 1
        assert last, "deeper stacks need a non-final variant of the second SwiGLU kernel"
        h = _ffn2_call(h, row(ffn2_norm[l]), ffn2_wg_b, ffn2_wu_b, ffn2_wd_b, row(final_norm))
    return h
```

```python
import functools
import math

import jax
import jax.numpy as jnp
from jax import lax
from jax.experimental import pallas as pl
from jax.experimental.pallas import tpu as pltpu

F32 = jnp.float32
BF16 = jnp.bfloat16

D_MODEL = 1024
N_MEM = 256
D_FF = 2816
D_POOL = 512
POOL_WINDOWS = (2, 4, 8, 16)
POOL_GROUP = 128
D_SSM = 256
SSM_GROUP = 16
N_SSM_GROUPS = 16
SSM_STATE = 64
D_STATE = N_SSM_GROUPS * SSM_STATE
N_XHEADS = 4
XHEAD_DIM = 256
EPS = 1e-6

V7X_VMEM_LIMIT_BYTES = 58 * 1024 * 1024
V7X_SUBLANES = 8
V7X_BF16_SUBLANES = 16
POOL_HALO = 16

FFN_TILE = 1024
MIX_TILE = 1024
SSM_TILE = 256
FFN_SKEW = 1
MIX_SKEW = 1
WEIGHT_CHUNKS = 16
WEIGHT_SLOTS = 4
GROUP_ROWS = 256
FF_CHUNKS = ((0, 768), (768, 1536), (1536, 2304), (2304, 2816))
SSM_SLOTS = 4
SSM_AHEAD = 2
SSM_SUBTILE = 16
SCAN_LANES = 512


def _rms(v, g):
    r = lax.rsqrt(jnp.mean(v * v, axis=-1, keepdims=True) + EPS)
    return (v * r) * g


def _dot(a, b):
    return jnp.dot(a, b, preferred_element_type=F32)


def _const_spec(shape):
    nd = len(shape)
    return pl.BlockSpec(shape, lambda *_: (0,) * nd)


def _kv_kernel(*refs, n_cast):
    (mem_ref, g_ref, wkv_ref), refs = refs[:3], refs[3:]
    cast_src, refs = refs[:n_cast], refs[n_cast:]
    (k_ref, v_ref), refs = refs[:2], refs[2:]
    cast_dst, (wkv_bf16,) = refs[:n_cast], refs[n_cast:]

    @pl.when(pl.program_id(0) == 0)
    def _():
        wkv_bf16[...] = wkv_ref[...].astype(BF16)

    m = _rms(mem_ref[...], g_ref[...]).astype(BF16)
    kv = _dot(m, wkv_bf16[...])
    k_ref[...] = kv[:, :D_MODEL].astype(BF16)
    v_ref[...] = kv[:, D_MODEL:].astype(BF16)
    _cast_blocks(cast_src, cast_dst)


def _kv_call(mem, mem_norm, w_kv, to_cast):
    B = mem.shape[0]
    per_batch = pl.BlockSpec((None, N_MEM, D_MODEL), lambda b: (b, 0, 0))
    cast_specs = _cast_specs(to_cast, B, lambda b: b)
    outs = pl.pallas_call(
        functools.partial(_kv_kernel, n_cast=len(to_cast)),
        grid=(B,),
        in_specs=[per_batch, _const_spec((1, D_MODEL)), _const_spec((D_MODEL, 2 * D_MODEL))] + cast_specs,
        out_specs=[per_batch, per_batch] + cast_specs,
        out_shape=[jax.ShapeDtypeStruct((B, N_MEM, D_MODEL), BF16)] * 2
                  + [jax.ShapeDtypeStruct(w.shape, BF16) for w in to_cast],
        scratch_shapes=[pltpu.VMEM((D_MODEL, 2 * D_MODEL), BF16)],
        compiler_params=pltpu.CompilerParams(
            dimension_semantics=("arbitrary",), vmem_limit_bytes=V7X_VMEM_LIMIT_BYTES),
        name="kv_proj",
    )(mem, mem_norm, w_kv, *to_cast)
    return outs[0], outs[1], outs[2:]


def _row_groups(n_rows):
    return [slice(r, r + GROUP_ROWS) for r in range(0, n_rows, GROUP_ROWS)]


def _trace_round_robin(chains, skew=0):
    waiting = list(chains)
    n_started = 0
    live = []
    turn = 0
    while waiting or live:
        while waiting and turn >= n_started * skew:
            live.append(waiting.pop(0))
            n_started += 1
        for c in list(live):
            try:
                next(c)
            except StopIteration:
                live.remove(c)
        turn += 1


def _load_weights_as_bf16(jobs, sem):
    chunks = []
    for src, dst, stage in jobs:
        rows = stage.shape[1]
        for r0 in range(0, src.shape[0], rows):
            chunks.append((src.at[pl.ds(r0, rows)], dst.at[pl.ds(r0, rows)], stage))
    ahead = WEIGHT_SLOTS

    def copy(k):
        src, _, stage = chunks[k]
        return pltpu.make_async_copy(src, stage.at[k % WEIGHT_SLOTS], sem.at[k % WEIGHT_SLOTS])

    for k in range(min(ahead, len(chunks))):
        copy(k).start()
    for k, (_, dst, stage) in enumerate(chunks):
        copy(k).wait()
        dst[...] = stage[k % WEIGHT_SLOTS].astype(BF16)
        if k + ahead < len(chunks):
            copy(k + ahead).start()


def _is_first_step_2d():
    return jnp.logical_and(pl.program_id(0) == 0, pl.program_id(1) == 0)


def _swiglu_residual(x, g, wg_ref, wu_ref, wd_ref):
    u = _rms(x, g).astype(BF16)
    acc = None
    for c0, c1 in FF_CHUNKS:
        gate = _dot(u, wg_ref[:, c0:c1])
        up = _dot(u, wu_ref[:, c0:c1])
        yield
        act = (gate * jax.nn.sigmoid(gate) * up).astype(BF16)
        part = _dot(act, wd_ref[c0:c1, :])
        acc = part if acc is None else acc + part
        yield
    return x + 0.5 * acc


_HBM = pl.BlockSpec(memory_space=pl.ANY)


def _swiglu_weight_scratch(D):
    return [
        pltpu.VMEM((D, D_FF), BF16),
        pltpu.VMEM((D, D_FF), BF16),
        pltpu.VMEM((D_FF, D), BF16),
        pltpu.VMEM((WEIGHT_SLOTS, _cast_block_rows(D, WEIGHT_CHUNKS), D_FF), F32),
        pltpu.VMEM((WEIGHT_SLOTS, _cast_block_rows(D_FF, WEIGHT_CHUNKS), D), F32),
        pltpu.SemaphoreType.DMA((WEIGHT_SLOTS,)),
    ]


def _cast_block_rows(n_rows, n_steps):
    rows = V7X_BF16_SUBLANES
    while n_rows % rows or n_rows // rows > n_steps:
        rows += V7X_BF16_SUBLANES
    return rows


def _cast_specs(to_cast, n_steps, step_of):
    specs = []
    for w in to_cast:
        rows = _cast_block_rows(w.shape[0], n_steps)
        last = w.shape[0] // rows - 1
        specs.append(pl.BlockSpec(
            (rows, w.shape[1]), lambda *idx, last=last: (jnp.minimum(step_of(*idx), last), 0)))
    return specs


def _cast_blocks(srcs, dsts):
    for src, dst in zip(srcs, dsts):
        dst[...] = src[...].astype(BF16)


def _ffn1_kernel(x_ref, n1_ref, wg_hbm, wu_hbm, wd_hbm, n2_ref, win_hbm, h_ref, p_ref, s_ref,
                 wg_ref, wu_ref, wd_ref, stage_up, stage_down, sem, wps_ref):
    @pl.when(_is_first_step_2d())
    def _():
        n_ps = D_POOL + D_SSM
        _load_weights_as_bf16([
            (wg_hbm, wg_ref, stage_up),
            (wu_hbm, wu_ref, stage_up),
            (wd_hbm, wd_ref, stage_down),
            (win_hbm.at[:, pl.ds(0, n_ps)], wps_ref, stage_up.at[:, :, pl.ds(0, n_ps)]),
        ], sem)

    def chain(rows):
        h = yield from _swiglu_residual(x_ref[rows, :], n1_ref[...], wg_ref, wu_ref, wd_ref)
        h_ref[rows, :] = h
        u = _rms(h, n2_ref[...]).astype(BF16)
        ps = _dot(u, wps_ref[...])
        p_ref[rows, :] = ps[:, :D_POOL]
        s_ref[rows, :] = ps[:, D_POOL:]

    _trace_round_robin([chain(rows) for rows in _row_groups(x_ref.shape[0])], skew=FFN_SKEW)


def _ffn1_call(x, n1, wg, wu, wd, n2, w_in):
    B, L, D = x.shape
    T = FFN_TILE
    tok = pl.BlockSpec((None, T, D), lambda b, i: (b, i, 0))
    return pl.pallas_call(
        _ffn1_kernel,
        grid=(B, L // T),
        in_specs=[tok, _const_spec((1, D)), _HBM, _HBM, _HBM, _const_spec((1, D)), _HBM],
        out_specs=[
            tok,
            pl.BlockSpec((None, T, D_POOL), lambda b, i: (b, i, 0)),
            pl.BlockSpec((None, T, D_SSM), lambda b, i: (b, i, 0)),
        ],
        out_shape=[
            jax.ShapeDtypeStruct((B, L, D), F32),
            jax.ShapeDtypeStruct((B, L, D_POOL), F32),
            jax.ShapeDtypeStruct((B, L, D_SSM), F32),
        ],
        scratch_shapes=_swiglu_weight_scratch(D) + [pltpu.VMEM((D, D_POOL + D_SSM), BF16)],
        compiler_params=pltpu.CompilerParams(
            dimension_semantics=("arbitrary", "arbitrary"), vmem_limit_bytes=V7X_VMEM_LIMIT_BYTES),
        name="ffn1",
    )(x, n1, wg, wu, wd, n2, w_in)


def _ffn2_kernel(h_ref, n_ref, wg_ref, wu_ref, wd_ref, fn_ref, o_ref):
    def chain(rows):
        h = yield from _swiglu_residual(h_ref[rows, :], n_ref[...], wg_ref, wu_ref, wd_ref)
        o_ref[rows, :] = _rms(h, fn_ref[...])

    _trace_round_robin([chain(rows) for rows in _row_groups(h_ref.shape[0])], skew=FFN_SKEW)


def _ffn2_call(h, n, wg, wu, wd, fn):
    B, L, D = h.shape
    T = FFN_TILE
    tok = pl.BlockSpec((None, T, D), lambda b, i: (b, i, 0))
    return pl.pallas_call(
        _ffn2_kernel,
        grid=(B, L // T),
        in_specs=[tok, _const_spec((1, D)), _const_spec((D, D_FF)), _const_spec((D, D_FF)),
                  _const_spec((D_FF, D)), _const_spec((1, D))],
        out_specs=tok,
        out_shape=jax.ShapeDtypeStruct((B, L, D), F32),
        compiler_params=pltpu.CompilerParams(
            dimension_semantics=("arbitrary", "arbitrary"), vmem_limit_bytes=V7X_VMEM_LIMIT_BYTES),
        name="ffn2",
    )(h, n, wg, wu, wd, fn)


def _ssm_kernel(*refs, n_cast):
    (sf_ref, sb_ref, are_ref, aim_ref, ldt_ref, bre_ref, bim_ref, crt_ref, cit_ref, d_ref), refs = refs[:10], refs[10:]
    cast_src, refs = refs[:n_cast], refs[n_cast:]
    (yf_ref, yb_ref), refs = refs[:2], refs[2:]
    cast_dst, refs = refs[:n_cast], refs[n_cast:]
    abar_scr, bmat_scr, cmat_scr, state_scr, bu_scr, xb_scr = refs
    i = pl.program_id(0)
    S = D_STATE
    n_b, steps, _ = sf_ref.shape
    R = steps * n_b

    @pl.when(i == 0)
    def _():
        for d in range(2):
            ar = are_ref[d]
            ai = aim_ref[d]
            dt = jnp.exp(ldt_ref[d])
            mag = jnp.exp(dt * ar)
            ang = dt * ai
            abr = mag * jnp.cos(ang)
            abi = mag * jnp.sin(ang)
            den = ar * ar + ai * ai
            nr = abr - 1.0
            qr = (nr * ar + abi * ai) / den
            qi = (abi * ar - nr * ai) / den
            abar_scr[d, 0] = jnp.broadcast_to(abr, (8, S))
            abar_scr[d, 1] = jnp.broadcast_to(abi, (8, S))
            br = bre_ref[d]
            bi = bim_ref[d]
            bmat_scr[d, :, :S] = (qr * br - qi * bi).astype(BF16)
            bmat_scr[d, :, S:] = (qr * bi + qi * br).astype(BF16)
            cmat_scr[d, :S, :] = crt_ref[d]
            cmat_scr[d, S:, :] = -cit_ref[d]
        state_scr[...] = jnp.zeros_like(state_scr)

    sub = SSM_SUBTILE
    n_sub = steps // sub
    halves = [(slice(c0, c0 + SCAN_LANES), slice(S + c0, S + c0 + SCAN_LANES))
              for c0 in range(0, S, SCAN_LANES)]
    def chain(d, s_ref, y_ref):
        order = list(range(n_sub)) if d == 0 else list(range(n_sub - 1, -1, -1))

        def slot_row(pos):
            return (pos % SSM_SLOTS) * sub * n_b

        def input_map(pos):
            j = order[pos]
            s_nat = s_ref[:, j * sub:(j + 1) * sub, :]
            s_tb = jnp.swapaxes(s_nat, 0, 1).reshape(sub * n_b, D_SSM)
            bu_scr[d, slot_row(pos):slot_row(pos) + sub * n_b, :] = _dot(s_tb.astype(BF16), bmat_scr[d])

        coef = [(abar_scr[d, 0, :, re], abar_scr[d, 1, :, re]) for re, _ in halves]
        state = [(state_scr[d, 0, :, re], state_scr[d, 1, :, re]) for re, _ in halves]
        for pos in range(min(SSM_AHEAD, n_sub)):
            input_map(pos)
        yield
        for pos, j in enumerate(order):
            if pos + SSM_AHEAD < n_sub:
                input_map(pos + SSM_AHEAD)
            r0 = slot_row(pos)
            for hf, (re, im) in enumerate(halves):
                ar, ai = coef[hf]
                xr, xi = state[hf]
                for k in range(0, sub, 2):
                    lo = k if d == 0 else sub - 2 - k
                    got = {}
                    for t in ((lo, lo + 1) if d == 0 else (lo + 1, lo)):
                        rows = slice(r0 + n_b * t, r0 + n_b * (t + 1))
                        nxr = ar * xr - ai * xi + bu_scr[d, rows, re]
                        nxi = ar * xi + ai * xr + bu_scr[d, rows, im]
                        xr, xi = nxr, nxi
                        got[t] = (nxr, nxi)
                    rows2 = slice(r0 + n_b * lo, r0 + n_b * (lo + 2))
                    xb_scr[d, rows2, re] = jnp.concatenate([got[lo][0], got[lo + 1][0]], axis=0).astype(BF16)
                    xb_scr[d, rows2, im] = jnp.concatenate([got[lo][1], got[lo + 1][1]], axis=0).astype(BF16)
                state[hf] = (xr, xi)
            y_tb = _dot(xb_scr[d, r0:r0 + sub * n_b, :], cmat_scr[d])
            y = jnp.swapaxes(y_tb.reshape(sub, n_b, D_SSM), 0, 1)
            if d == 0:
                y = y + sf_ref[:, j * sub:(j + 1) * sub, :] * d_ref[...]
            y_ref[:, j * sub:(j + 1) * sub, :] = y
            yield
        for hf, (re, _) in enumerate(halves):
            state_scr[d, 0, :, re] = state[hf][0]
            state_scr[d, 1, :, re] = state[hf][1]

    _trace_round_robin([chain(0, sf_ref, yf_ref), chain(1, sb_ref, yb_ref)])
    _cast_blocks(cast_src, cast_dst)


def _ssm_call(s, a_re, a_im, log_dt, b_re_bd, b_im_bd, c_re_t, c_im_t, d_skip, to_cast):
    B, L, _ = s.shape
    assert B == V7X_SUBLANES, "the scan keeps one batch element per sublane"
    T = SSM_TILE
    n = L // T
    S = D_STATE
    fwd = pl.BlockSpec((B, T, D_SSM), lambda i: (0, i, 0))
    bwd = pl.BlockSpec((B, T, D_SSM), lambda i: (0, n - 1 - i, 0))
    cast_specs = _cast_specs(to_cast, n, lambda i: i)
    outs = pl.pallas_call(
        functools.partial(_ssm_kernel, n_cast=len(to_cast)),
        grid=(n,),
        in_specs=[
            fwd, bwd,
            _const_spec((2, 1, S)), _const_spec((2, 1, S)), _const_spec((2, 1, S)),
            _const_spec((2, D_SSM, S)), _const_spec((2, D_SSM, S)),
            _const_spec((2, S, D_SSM)), _const_spec((2, S, D_SSM)),
            _const_spec((1, D_SSM)),
        ] + cast_specs,
        out_specs=[fwd, bwd] + cast_specs,
        out_shape=[jax.ShapeDtypeStruct((B, L, D_SSM), F32)] * 2
                  + [jax.ShapeDtypeStruct(w.shape, BF16) for w in to_cast],
        scratch_shapes=[
            pltpu.VMEM((2, 2, 8, S), F32),
            pltpu.VMEM((2, D_SSM, 2 * S), BF16),
            pltpu.VMEM((2, 2 * S, D_SSM), BF16),
            pltpu.VMEM((2, 2, 8, S), F32),
            pltpu.VMEM((2, SSM_SLOTS * SSM_SUBTILE * B, 2 * S), F32),
            pltpu.VMEM((2, SSM_SLOTS * SSM_SUBTILE * B, 2 * S), BF16),
        ],
        compiler_params=pltpu.CompilerParams(
            dimension_semantics=("arbitrary",), vmem_limit_bytes=V7X_VMEM_LIMIT_BYTES),
        name="ssm_scan",
    )(s, s, a_re, a_im, log_dt, b_re_bd, b_im_bd, c_re_t, c_im_t, d_skip, *to_cast)
    return outs[0], outs[1], outs[2:]


def _window_sum(v, w):
    n = v.shape[0]
    ahead = lambda a, k: pltpu.roll(a, n - k, axis=0)
    behind = lambda a, k: pltpu.roll(a, k, axis=0)
    if w == 2:
        return v + behind(v, 1)
    cur, span = v + ahead(v, 1), 2
    while span < w // 2:
        cur, span = cur + ahead(cur, span), 2 * span
    return cur + behind(cur, span)


def _mix_kernel(h_ref, p_ref, pl_ref, pr_ref, yf_ref, yb_ref, k_ref, v_ref,
                nmix_ref, win_ref, poolw_ref, pscale_ref, wpp_ref, wgluv_ref, wglug_ref, wmo_ref,
                nx_ref, wq_ref, wxo_ref, o_ref, *, seq_len):
    i = pl.program_id(1)
    n_i = pl.num_programs(1)
    T = h_ref.shape[0]
    H = POOL_HALO

    halo_l = jnp.where(i > 0, pl_ref[...], 0.0)
    halo_r = jnp.where(i < n_i - 1, pr_ref[...], 0.0)
    edge = V7X_SUBLANES
    t_edge = lax.broadcasted_iota(jnp.int32, (edge, POOL_GROUP), 0)
    pooled_groups = []
    for g, w in enumerate(POOL_WINDOWS):
        left = w // 2
        right = w - 1 - left
        lanes = slice(g * POOL_GROUP, (g + 1) * POOL_GROUP)
        v = jnp.concatenate([halo_l[:, lanes], p_ref[:, lanes], halo_r[:, lanes]], axis=0)
        tot = _window_sum(v, w)[H:H + T]
        self_ = v[H:H + T]

        def clipped(t0, rows):
            t = i * T + t0 + t_edge
            cnt = jnp.minimum(t + (right + 1), seq_len) - jnp.maximum(t - left, 0)
            return tot[rows] / cnt.astype(F32) - self_[rows]

        pooled_groups.append(jnp.concatenate([
            clipped(0, slice(0, edge)),
            tot[edge:T - edge] * (1.0 / w) - self_[edge:T - edge],
            clipped(T - edge, slice(T - edge, T)),
        ], axis=0).astype(BF16))

    def chain(rows):
        h = h_ref[rows, :]
        mixed = [_dot(pooled_groups[g][rows], poolw_ref[g]) for g in range(len(POOL_WINDOWS))]
        yield
        mixed = jnp.concatenate(mixed, axis=-1) * pscale_ref[...]
        z_pool = _dot(mixed.astype(BF16), wpp_ref[...])
        yield

        y = jax.nn.gelu(yf_ref[rows, :] + yb_ref[rows, :]).astype(BF16)
        glu_val = _dot(y, wgluv_ref[...])
        glu_gate = _dot(y, wglug_ref[...])
        yield
        z_ssm = glu_val * jax.nn.sigmoid(glu_gate)

        u = _rms(h, nmix_ref[...]).astype(BF16)
        gates = _dot(u, win_ref[:, D_POOL + D_SSM:])
        yield
        merged = (jax.nn.sigmoid(gates[:, :D_MODEL]) * z_pool
                  + jax.nn.sigmoid(gates[:, D_MODEL:]) * z_ssm)
        h = h + _dot(merged.astype(BF16), wmo_ref[...])
        yield

        q = _dot(_rms(h, nx_ref[...]).astype(BF16), wq_ref[...])
        yield
        head_cols = [slice(hd * XHEAD_DIM, (hd + 1) * XHEAD_DIM) for hd in range(N_XHEADS)]
        scores = [lax.dot_general(q[:, cols].astype(BF16), k_ref[:, cols], (((1,), (1,)), ((), ())),
                                  preferred_element_type=F32) / math.sqrt(XHEAD_DIM) for cols in head_cols]
        yield
        heads = []
        for sc, cols in zip(scores, head_cols):
            e = jnp.exp(sc - jnp.max(sc, axis=-1, keepdims=True))
            probs = e / jnp.sum(e, axis=-1, keepdims=True)
            heads.append(_dot(probs.astype(BF16), v_ref[:, cols]))
        yield
        o = jnp.concatenate(heads, axis=-1).astype(BF16)
        o_ref[rows, :] = h + _dot(o, wxo_ref[...])

    _trace_round_robin([chain(rows) for rows in _row_groups(T)], skew=MIX_SKEW)


def _mix_call(h, p, yf, yb, k, v, nmix, wgate, poolw, pscale, wpp, wgluv, wglug, wmo, nx, wq, wxo):
    B, L, D = h.shape
    T = MIX_TILE
    H = POOL_HALO
    per_tile = T // H
    n_halo = L // H
    tok = pl.BlockSpec((None, T, D), lambda b, i: (b, i, 0))
    y_spec = pl.BlockSpec((None, T, D_SSM), lambda b, i: (b, i, 0))
    mem_spec = pl.BlockSpec((None, N_MEM, D), lambda b, i: (b, 0, 0))
    return pl.pallas_call(
        functools.partial(_mix_kernel, seq_len=L),
        grid=(B, L // T),
        in_specs=[
            tok,
            pl.BlockSpec((None, T, D_POOL), lambda b, i: (b, i, 0)),
            pl.BlockSpec((None, H, D_POOL), lambda b, i: (b, jnp.maximum(i * per_tile - 1, 0), 0)),
            pl.BlockSpec((None, H, D_POOL), lambda b, i: (b, jnp.minimum((i + 1) * per_tile, n_halo - 1), 0)),
            y_spec, y_spec,
            mem_spec, mem_spec,
            _const_spec((1, D)),
            _const_spec((D, D_POOL + D_SSM + 2 * D)),
            _const_spec((len(POOL_WINDOWS), POOL_GROUP, POOL_GROUP)),
            _const_spec((1, D_POOL)),
            _const_spec((D_POOL, D)),
            _const_spec((D_SSM, D)),
            _const_spec((D_SSM, D)),
            _const_spec((D, D)),
            _const_spec((1, D)),
            _const_spec((D, D)),
            _const_spec((D, D)),
        ],
        out_specs=tok,
        out_shape=jax.ShapeDtypeStruct((B, L, D), F32),
        compiler_params=pltpu.CompilerParams(
            dimension_semantics=("arbitrary", "arbitrary"), vmem_limit_bytes=V7X_VMEM_LIMIT_BYTES),
        name="mix_attn",
    )(h, p, p, p, yf, yb, k, v, nmix, wgate, poolw, pscale, wpp, wgluv, wglug, wmo, nx, wq, wxo)


def _block_diag_in(w):
    G, P, Hh = w.shape
    per_channel = jnp.transpose(w, (2, 0, 1)).reshape(Hh, G * P)
    same_group = (jnp.arange(G * Hh)[:, None] // Hh) == (jnp.arange(G * P)[None, :] // P)
    return jnp.where(same_group, jnp.tile(per_channel, (G, 1)), 0.0)


def _block_diag_out(w):
    G, Hh, P = w.shape
    per_state = jnp.transpose(w, (0, 2, 1)).reshape(G * P, Hh)
    same_group = (jnp.arange(G * P)[:, None] // P) == (jnp.arange(G * Hh)[None, :] // Hh)
    return jnp.where(same_group, jnp.tile(per_state, (1, G)), 0.0)


def kernel(x, mem, ffn1_norm, ffn1_w_gate, ffn1_w_up, ffn1_w_down, mix_norm, w_in, pool_w, pool_scale,
           w_pool_proj, ssm_a_re, ssm_a_im, ssm_log_dt, ssm_b_re, ssm_b_im, ssm_c_re, ssm_c_im, ssm_d,
           w_glu_val, w_glu_gate, w_mix_out, xattn_norm, mem_norm, w_q, w_kv, w_xo, ffn2_norm,
           ffn2_w_gate, ffn2_w_up, ffn2_w_down, final_norm):
    B, L, D = x.shape
    depth = ffn1_norm.shape[0]
    row = lambda v: v.reshape(1, -1)
    bf = lambda w: w.astype(BF16)
    h = x
    for l in range(depth):
        k_mem, v_mem, cast = _kv_call(
            mem, row(mem_norm[l]), w_kv[l],
            [w_in[l], pool_w[l].reshape(-1, POOL_GROUP), w_pool_proj[l], w_glu_val[l], w_glu_gate[l],
             w_mix_out[l], w_q[l], w_xo[l]])
        w_in_b, pool_w_b, w_pp_b, w_gluv_b, w_glug_b, w_mo_b, w_q_b, w_xo_b = cast

        h, p, s = _ffn1_call(
            h, row(ffn1_norm[l]), ffn1_w_gate[l], ffn1_w_up[l], ffn1_w_down[l], row(mix_norm[l]), w_in[l])

        G, P = N_SSM_GROUPS, SSM_STATE
        per_state = lambda a: a.reshape(2, 1, G * P)
        log_dt = jnp.broadcast_to(ssm_log_dt[l][:, :, None], (2, G, P))
        yf, yb, (ffn2_wg_b, ffn2_wu_b, ffn2_wd_b) = _ssm_call(
            s, per_state(ssm_a_re[l]), per_state(ssm_a_im[l]), per_state(log_dt),
            jax.vmap(_block_diag_in)(ssm_b_re[l]), jax.vmap(_block_diag_in)(ssm_b_im[l]),
            bf(jax.vmap(_block_diag_out)(ssm_c_re[l])), bf(jax.vmap(_block_diag_out)(ssm_c_im[l])),
            row(ssm_d[l]), [ffn2_w_gate[l], ffn2_w_up[l], ffn2_w_down[l]])

        h = _mix_call(
            h, p, yf, yb, k_mem, v_mem,
            row(mix_norm[l]), w_in_b, pool_w_b.reshape(pool_w[l].shape), row(pool_scale[l]),
            w_pp_b, w_gluv_b, w_glug_b, w_mo_b, row(xattn_norm[l]), w_q_b, w_xo_b)

        last = l == depth - 1
        assert last, "deeper stacks need a non-final variant of the second SwiGLU kernel"
        h = _ffn2_call(h, row(ffn2_norm[l]), ffn2_wg_b, ffn2_wu_b, ffn2_wd_b, row(final_norm))
    return h
```

```python
import functools
import math

import jax
import jax.numpy as jnp
from jax import lax
from jax.experimental import pallas as pl
from jax.experimental.pallas import tpu as pltpu

F32 = jnp.float32
BF16 = jnp.bfloat16

D_MODEL = 1024
N_MEM = 256
D_FF = 2816
D_POOL = 512
POOL_WINDOWS = (2, 4, 8, 16)
POOL_GROUP = 128
D_SSM = 256
N_SSM_GROUPS = 16
SSM_STATE = 64
D_STATE = N_SSM_GROUPS * SSM_STATE
N_XHEADS = 4
XHEAD_DIM = 256
EPS = 1e-6

V7X_VMEM_LIMIT_BYTES = 58 * 1024 * 1024
V7X_SUBLANES = 8
V7X_BF16_SUBLANES = 16
POOL_HALO = 16

FFN_TILE = 1024
MIX_TILE = 1024
SSM_TILE = 256
MIX_SKEW = 1
FFN_SKEW = 1
WEIGHT_CHUNKS = 16
WEIGHT_SLOTS = 4
GROUP_ROWS = 256
FF_CHUNKS = ((0, 768), (768, 1536), (1536, 2304), (2304, 2816))
SSM_SLOTS = 4
SSM_AHEAD = 2
SSM_SUBTILE = 16
SCAN_LANES = 512


def _rms(v, g):
    r = lax.rsqrt(jnp.mean(v * v, axis=-1, keepdims=True) + EPS)
    return (v * r) * g


def _dot(a, b):
    return jnp.dot(a, b, preferred_element_type=F32)


def _const_spec(shape):
    nd = len(shape)
    return pl.BlockSpec(shape, lambda *_: (0,) * nd)


def _kv_kernel(*refs, n_cast):
    (mem_ref, g_ref, wkv_ref), refs = refs[:3], refs[3:]
    cast_src, refs = refs[:n_cast], refs[n_cast:]
    (k_ref, v_ref), refs = refs[:2], refs[2:]
    cast_dst, (wkv_bf16,) = refs[:n_cast], refs[n_cast:]

    @pl.when(pl.program_id(0) == 0)
    def _():
        wkv_bf16[...] = wkv_ref[...].astype(BF16)

    m = _rms(mem_ref[...], g_ref[...]).astype(BF16)
    kv = _dot(m, wkv_bf16[...])
    k_ref[...] = kv[:, :D_MODEL].astype(BF16)
    v_ref[...] = kv[:, D_MODEL:].astype(BF16)
    _cast_blocks(cast_src, cast_dst)


def _kv_call(mem, mem_norm, w_kv, to_cast):
    B = mem.shape[0]
    per_batch = pl.BlockSpec((None, N_MEM, D_MODEL), lambda b: (b, 0, 0))
    cast_specs = _cast_specs(to_cast, B, lambda b: b)
    outs = pl.pallas_call(
        functools.partial(_kv_kernel, n_cast=len(to_cast)),
        grid=(B,),
        in_specs=[per_batch, _const_spec((1, D_MODEL)), _const_spec((D_MODEL, 2 * D_MODEL))] + cast_specs,
        out_specs=[per_batch, per_batch] + cast_specs,
        out_shape=[jax.ShapeDtypeStruct((B, N_MEM, D_MODEL), BF16)] * 2
                  + [jax.ShapeDtypeStruct(w.shape, BF16) for w in to_cast],
        scratch_shapes=[pltpu.VMEM((D_MODEL, 2 * D_MODEL), BF16)],
        compiler_params=pltpu.CompilerParams(
            dimension_semantics=("arbitrary",), vmem_limit_bytes=V7X_VMEM_LIMIT_BYTES),
        name="kv_proj",
    )(mem, mem_norm, w_kv, *to_cast)
    return outs[0], outs[1], outs[2:]


def _row_groups(n_rows):
    return [slice(r, r + GROUP_ROWS) for r in range(0, n_rows, GROUP_ROWS)]


def _trace_round_robin(chains, skew=0):
    waiting = list(chains)
    n_started = 0
    live = []
    turn = 0
    while waiting or live:
        while waiting and turn >= n_started * skew:
            live.append(waiting.pop(0))
            n_started += 1
        for c in list(live):
            try:
                next(c)
            except StopIteration:
                live.remove(c)
        turn += 1


def _load_weights_as_bf16(jobs, sem):
    chunks = []
    for src, dst, stage in jobs:
        rows = stage.shape[1]
        for r0 in range(0, src.shape[0], rows):
            chunks.append((src.at[pl.ds(r0, rows)], dst.at[pl.ds(r0, rows)], stage))
    ahead = WEIGHT_SLOTS

    def copy(k):
        src, _, stage = chunks[k]
        return pltpu.make_async_copy(src, stage.at[k % WEIGHT_SLOTS], sem.at[k % WEIGHT_SLOTS])

    for k in range(min(ahead, len(chunks))):
        copy(k).start()
    for k, (_, dst, stage) in enumerate(chunks):
        copy(k).wait()
        dst[...] = stage[k % WEIGHT_SLOTS].astype(BF16)
        if k + ahead < len(chunks):
            copy(k + ahead).start()


def _is_first_step_2d():
    return jnp.logical_and(pl.program_id(0) == 0, pl.program_id(1) == 0)


def _swiglu_residual(x, g, wg_ref, wu_ref, wd_ref):
    u = _rms(x, g).astype(BF16)
    acc = None
    for c0, c1 in FF_CHUNKS:
        gate = _dot(u, wg_ref[:, c0:c1])
        up = _dot(u, wu_ref[:, c0:c1])
        yield
        act = (gate * jax.nn.sigmoid(gate) * up).astype(BF16)
        part = _dot(act, wd_ref[c0:c1, :])
        acc = part if acc is None else acc + part
        yield
    return x + 0.5 * acc


_HBM = pl.BlockSpec(memory_space=pl.ANY)


def _swiglu_weight_scratch(D):
    return [
        pltpu.VMEM((D, D_FF), BF16),
        pltpu.VMEM((D, D_FF), BF16),
        pltpu.VMEM((D_FF, D), BF16),
        pltpu.VMEM((WEIGHT_SLOTS, _cast_block_rows(D, WEIGHT_CHUNKS), D_FF), F32),
        pltpu.VMEM((WEIGHT_SLOTS, _cast_block_rows(D_FF, WEIGHT_CHUNKS), D), F32),
        pltpu.SemaphoreType.DMA((WEIGHT_SLOTS,)),
    ]


def _cast_block_rows(n_rows, n_steps):
    rows = V7X_BF16_SUBLANES
    while n_rows % rows or n_rows // rows > n_steps:
        rows += V7X_BF16_SUBLANES
    return rows


def _cast_specs(to_cast, n_steps, step_of):
    specs = []
    for w in to_cast:
        rows = _cast_block_rows(w.shape[0], n_steps)
        last = w.shape[0] // rows - 1
        specs.append(pl.BlockSpec(
            (rows, w.shape[1]), lambda *idx, last=last: (jnp.minimum(step_of(*idx), last), 0)))
    return specs


def _cast_blocks(srcs, dsts):
    for src, dst in zip(srcs, dsts):
        dst[...] = src[...].astype(BF16)


def _ffn1_kernel(x_ref, n1_ref, wg_hbm, wu_hbm, wd_hbm, n2_ref, win_hbm, h_ref, p_ref, s_ref,
                 wg_ref, wu_ref, wd_ref, stage_up, stage_down, sem, wps_ref):
    @pl.when(_is_first_step_2d())
    def _():
        n_ps = D_POOL + D_SSM
        _load_weights_as_bf16([
            (wg_hbm, wg_ref, stage_up),
            (wu_hbm, wu_ref, stage_up),
            (wd_hbm, wd_ref, stage_down),
            (win_hbm.at[:, pl.ds(0, n_ps)], wps_ref, stage_up.at[:, :, pl.ds(0, n_ps)]),
        ], sem)

    def chain(rows):
        h = yield from _swiglu_residual(x_ref[rows, :], n1_ref[...], wg_ref, wu_ref, wd_ref)
        h_ref[rows, :] = h
        u = _rms(h, n2_ref[...]).astype(BF16)
        ps = _dot(u, wps_ref[...])
        p_ref[rows, :] = ps[:, :D_POOL]
        s_ref[rows, :] = ps[:, D_POOL:]

    _trace_round_robin([chain(rows) for rows in _row_groups(x_ref.shape[0])], skew=FFN_SKEW)


def _ffn1_call(x, n1, wg, wu, wd, n2, w_in):
    B, L, D = x.shape
    T = FFN_TILE
    tok = pl.BlockSpec((None, T, D), lambda b, i: (b, i, 0))
    return pl.pallas_call(
        _ffn1_kernel,
        grid=(B, L // T),
        in_specs=[tok, _const_spec((1, D)), _HBM, _HBM, _HBM, _const_spec((1, D)), _HBM],
        out_specs=[
            tok,
            pl.BlockSpec((None, T, D_POOL), lambda b, i: (b, i, 0)),
            pl.BlockSpec((None, T, D_SSM), lambda b, i: (b, i, 0)),
        ],
        out_shape=[
            jax.ShapeDtypeStruct((B, L, D), F32),
            jax.ShapeDtypeStruct((B, L, D_POOL), F32),
            jax.ShapeDtypeStruct((B, L, D_SSM), F32),
        ],
        scratch_shapes=_swiglu_weight_scratch(D) + [pltpu.VMEM((D, D_POOL + D_SSM), BF16)],
        compiler_params=pltpu.CompilerParams(
            dimension_semantics=("arbitrary", "arbitrary"), vmem_limit_bytes=V7X_VMEM_LIMIT_BYTES),
        name="ffn1",
    )(x, n1, wg, wu, wd, n2, w_in)


def _ffn2_kernel(h_ref, n_ref, wg_ref, wu_ref, wd_ref, fn_ref, o_ref):
    def chain(rows):
        h = yield from _swiglu_residual(h_ref[rows, :], n_ref[...], wg_ref, wu_ref, wd_ref)
        o_ref[rows, :] = _rms(h, fn_ref[...])

    _trace_round_robin([chain(rows) for rows in _row_groups(h_ref.shape[0])], skew=FFN_SKEW)


def _ffn2_call(h, n, wg, wu, wd, fn):
    B, L, D = h.shape
    T = FFN_TILE
    tok = pl.BlockSpec((None, T, D), lambda b, i: (b, i, 0))
    return pl.pallas_call(
        _ffn2_kernel,
        grid=(B, L // T),
        in_specs=[tok, _const_spec((1, D)), _const_spec((D, D_FF)), _const_spec((D, D_FF)),
                  _const_spec((D_FF, D)), _const_spec((1, D))],
        out_specs=tok,
        out_shape=jax.ShapeDtypeStruct((B, L, D), F32),
        compiler_params=pltpu.CompilerParams(
            dimension_semantics=("arbitrary", "arbitrary"), vmem_limit_bytes=V7X_VMEM_LIMIT_BYTES),
        name="ffn2",
    )(h, n, wg, wu, wd, fn)


def _ssm_kernel(*refs, n_cast):
    (sf_ref, sb_ref, are_ref, aim_ref, ldt_ref, bre_ref, bim_ref, crt_ref, cit_ref, d_ref), refs = refs[:10], refs[10:]
    cast_src, refs = refs[:n_cast], refs[n_cast:]
    (yf_ref, yb_ref), refs = refs[:2], refs[2:]
    cast_dst, refs = refs[:n_cast], refs[n_cast:]
    abar_scr, bmat_scr, cmat_scr, state_scr, bu_scr, xb_scr = refs
    i = pl.program_id(0)
    S = D_STATE
    n_b, steps, _ = sf_ref.shape

    @pl.when(i == 0)
    def _():
        for d in range(2):
            ar = are_ref[d]
            ai = aim_ref[d]
            dt = jnp.exp(ldt_ref[d])
            mag = jnp.exp(dt * ar)
            ang = dt * ai
            abr = mag * jnp.cos(ang)
            abi = mag * jnp.sin(ang)
            den = ar * ar + ai * ai
            nr = abr - 1.0
            qr = (nr * ar + abi * ai) / den
            qi = (abi * ar - nr * ai) / den
            abar_scr[d, 0] = jnp.broadcast_to(abr, (n_b, S))
            abar_scr[d, 1] = jnp.broadcast_to(abi, (n_b, S))
            br = bre_ref[d]
            bi = bim_ref[d]
            bmat_scr[d, :, :S] = (qr * br - qi * bi).astype(BF16)
            bmat_scr[d, :, S:] = (qr * bi + qi * br).astype(BF16)
            cmat_scr[d, :S, :] = crt_ref[d]
            cmat_scr[d, S:, :] = -cit_ref[d]
        state_scr[...] = jnp.zeros_like(state_scr)

    sub = SSM_SUBTILE
    n_sub = steps // sub
    halves = [(slice(c0, c0 + SCAN_LANES), slice(S + c0, S + c0 + SCAN_LANES))
              for c0 in range(0, S, SCAN_LANES)]
    def chain(d, s_ref, y_ref):
        order = list(range(n_sub)) if d == 0 else list(range(n_sub - 1, -1, -1))

        def slot_row(pos):
            return (pos % SSM_SLOTS) * sub * n_b

        def input_map(pos):
            j = order[pos]
            s_nat = s_ref[:, j * sub:(j + 1) * sub, :]
            s_tb = jnp.swapaxes(s_nat, 0, 1).reshape(sub * n_b, D_SSM)
            bu_scr[d, slot_row(pos):slot_row(pos) + sub * n_b, :] = _dot(s_tb.astype(BF16), bmat_scr[d])

        coef = [(abar_scr[d, 0, :, re], abar_scr[d, 1, :, re]) for re, _ in halves]
        state = [(state_scr[d, 0, :, re], state_scr[d, 1, :, re]) for re, _ in halves]
        for pos in range(min(SSM_AHEAD, n_sub)):
            input_map(pos)
        yield
        for pos, j in enumerate(order):
            if pos + SSM_AHEAD < n_sub:
                input_map(pos + SSM_AHEAD)
            r0 = slot_row(pos)
            for hf, (re, im) in enumerate(halves):
                ar, ai = coef[hf]
                xr, xi = state[hf]
                for k in range(0, sub, 2):
                    lo = k if d == 0 else sub - 2 - k
                    got = {}
                    for t in ((lo, lo + 1) if d == 0 else (lo + 1, lo)):
                        rows = slice(r0 + n_b * t, r0 + n_b * (t + 1))
                        nxr = ar * xr - ai * xi + bu_scr[d, rows, re]
                        nxi = ar * xi + ai * xr + bu_scr[d, rows, im]
                        xr, xi = nxr, nxi
                        got[t] = (nxr, nxi)
                    rows2 = slice(r0 + n_b * lo, r0 + n_b * (lo + 2))
                    xb_scr[d, rows2, re] = jnp.concatenate([got[lo][0], got[lo + 1][0]], axis=0).astype(BF16)
                    xb_scr[d, rows2, im] = jnp.concatenate([got[lo][1], got[lo + 1][1]], axis=0).astype(BF16)
                state[hf] = (xr, xi)
            y_tb = _dot(xb_scr[d, r0:r0 + sub * n_b, :], cmat_scr[d])
            y = jnp.swapaxes(y_tb.reshape(sub, n_b, D_SSM), 0, 1)
            if d == 0:
                y = y + sf_ref[:, j * sub:(j + 1) * sub, :] * d_ref[...]
            y_ref[:, j * sub:(j + 1) * sub, :] = y
            yield
        for hf, (re, _) in enumerate(halves):
            state_scr[d, 0, :, re] = state[hf][0]
            state_scr[d, 1, :, re] = state[hf][1]

    _trace_round_robin([chain(0, sf_ref, yf_ref), chain(1, sb_ref, yb_ref)])
    _cast_blocks(cast_src, cast_dst)


def _ssm_call(s, a_re, a_im, log_dt, b_re_bd, b_im_bd, c_re_t, c_im_t, d_skip, to_cast):
    B, L, _ = s.shape
    assert B == V7X_SUBLANES, "the scan keeps one batch element per sublane"
    T = SSM_TILE
    n = L // T
    S = D_STATE
    fwd = pl.BlockSpec((B, T, D_SSM), lambda i: (0, i, 0))
    bwd = pl.BlockSpec((B, T, D_SSM), lambda i: (0, n - 1 - i, 0))
    cast_specs = _cast_specs(to_cast, n, lambda i: i)
    outs = pl.pallas_call(
        functools.partial(_ssm_kernel, n_cast=len(to_cast)),
        grid=(n,),
        in_specs=[
            fwd, bwd,
            _const_spec((2, 1, S)), _const_spec((2, 1, S)), _const_spec((2, 1, S)),
            _const_spec((2, D_SSM, S)), _const_spec((2, D_SSM, S)),
            _const_spec((2, S, D_SSM)), _const_spec((2, S, D_SSM)),
            _const_spec((1, D_SSM)),
        ] + cast_specs,
        out_specs=[fwd, bwd] + cast_specs,
        out_shape=[jax.ShapeDtypeStruct((B, L, D_SSM), F32)] * 2
                  + [jax.ShapeDtypeStruct(w.shape, BF16) for w in to_cast],
        scratch_shapes=[
            pltpu.VMEM((2, 2, B, S), F32),
            pltpu.VMEM((2, D_SSM, 2 * S), BF16),
            pltpu.VMEM((2, 2 * S, D_SSM), BF16),
            pltpu.VMEM((2, 2, B, S), F32),
            pltpu.VMEM((2, SSM_SLOTS * SSM_SUBTILE * B, 2 * S), F32),
            pltpu.VMEM((2, SSM_SLOTS * SSM_SUBTILE * B, 2 * S), BF16),
        ],
        compiler_params=pltpu.CompilerParams(
            dimension_semantics=("arbitrary",), vmem_limit_bytes=V7X_VMEM_LIMIT_BYTES),
        name="ssm_scan",
    )(s, s, a_re, a_im, log_dt, b_re_bd, b_im_bd, c_re_t, c_im_t, d_skip, *to_cast)
    return outs[0], outs[1], outs[2:]


def _window_sum(v, w):
    n = v.shape[0]
    ahead = lambda a, k: pltpu.roll(a, n - k, axis=0)
    behind = lambda a, k: pltpu.roll(a, k, axis=0)
    if w == 2:
        return v + behind(v, 1)
    cur, span = v + ahead(v, 1), 2
    while span < w // 2:
        cur, span = cur + ahead(cur, span), 2 * span
    return cur + behind(cur, span)


def _mix_kernel(h_ref, p_ref, pl_ref, pr_ref, yf_ref, yb_ref, k_ref, v_ref,
                nmix_ref, win_ref, poolw_ref, pscale_ref, wpp_ref, wgluv_ref, wglug_ref, wmo_ref,
                nx_ref, wq_ref, wxo_ref, o_ref, *, seq_len):
    i = pl.program_id(1)
    n_i = pl.num_programs(1)
    T = h_ref.shape[0]
    H = POOL_HALO

    halo_l = jnp.where(i > 0, pl_ref[...], 0.0)
    halo_r = jnp.where(i < n_i - 1, pr_ref[...], 0.0)
    edge = V7X_SUBLANES
    t_edge = lax.broadcasted_iota(jnp.int32, (edge, POOL_GROUP), 0)
    pooled_groups = []
    for g, w in enumerate(POOL_WINDOWS):
        left = w // 2
        right = w - 1 - left
        lanes = slice(g * POOL_GROUP, (g + 1) * POOL_GROUP)
        v = jnp.concatenate([halo_l[:, lanes], p_ref[:, lanes], halo_r[:, lanes]], axis=0)
        tot = _window_sum(v, w)[H:H + T]
        self_ = v[H:H + T]

        def clipped(t0, rows):
            t = i * T + t0 + t_edge
            cnt = jnp.minimum(t + (right + 1), seq_len) - jnp.maximum(t - left, 0)
            return tot[rows] / cnt.astype(F32) - self_[rows]

        pooled_groups.append(jnp.concatenate([
            clipped(0, slice(0, edge)),
            tot[edge:T - edge] * (1.0 / w) - self_[edge:T - edge],
            clipped(T - edge, slice(T - edge, T)),
        ], axis=0).astype(BF16))

    def chain(rows):
        h = h_ref[rows, :]
        mixed = [_dot(pooled_groups[g][rows], poolw_ref[g]) for g in range(len(POOL_WINDOWS))]
        yield
        mixed = jnp.concatenate(mixed, axis=-1) * pscale_ref[...]
        z_pool = _dot(mixed.astype(BF16), wpp_ref[...])
        yield

        y = jax.nn.gelu(yf_ref[rows, :] + yb_ref[rows, :]).astype(BF16)
        glu_val = _dot(y, wgluv_ref[...])
        glu_gate = _dot(y, wglug_ref[...])
        yield
        z_ssm = glu_val * jax.nn.sigmoid(glu_gate)

        u = _rms(h, nmix_ref[...]).astype(BF16)
        gates = _dot(u, win_ref[:, D_POOL + D_SSM:])
        yield
        merged = (jax.nn.sigmoid(gates[:, :D_MODEL]) * z_pool
                  + jax.nn.sigmoid(gates[:, D_MODEL:]) * z_ssm)
        h = h + _dot(merged.astype(BF16), wmo_ref[...])
        yield

        q = _dot(_rms(h, nx_ref[...]).astype(BF16), wq_ref[...])
        yield
        head_cols = [slice(hd * XHEAD_DIM, (hd + 1) * XHEAD_DIM) for hd in range(N_XHEADS)]
        scores = [lax.dot_general(q[:, cols].astype(BF16), k_ref[:, cols], (((1,), (1,)), ((), ())),
                                  preferred_element_type=F32) / math.sqrt(XHEAD_DIM) for cols in head_cols]
        yield
        heads = []
        for sc, cols in zip(scores, head_cols):
            e = jnp.exp(sc - jnp.max(sc, axis=-1, keepdims=True))
            probs = e / jnp.sum(e, axis=-1, keepdims=True)
            heads.append(_dot(probs.astype(BF16), v_ref[:, cols]))
        yield
        o = jnp.concatenate(heads, axis=-1).astype(BF16)
        o_ref[rows, :] = h + _dot(o, wxo_ref[...])

    _trace_round_robin([chain(rows) for rows in _row_groups(T)], skew=MIX_SKEW)


def _mix_call(h, p, yf, yb, k, v, nmix, wgate, poolw, pscale, wpp, wgluv, wglug, wmo, nx, wq, wxo):
    B, L, D = h.shape
    T = MIX_TILE
    H = POOL_HALO
    per_tile = T // H
    n_halo = L // H
    tok = pl.BlockSpec((None, T, D), lambda b, i: (b, i, 0))
    y_spec = pl.BlockSpec((None, T, D_SSM), lambda b, i: (b, i, 0))
    mem_spec = pl.BlockSpec((None, N_MEM, D), lambda b, i: (b, 0, 0))
    return pl.pallas_call(
        functools.partial(_mix_kernel, seq_len=L),
        grid=(B, L // T),
        in_specs=[
            tok,
            pl.BlockSpec((None, T, D_POOL), lambda b, i: (b, i, 0)),
            pl.BlockSpec((None, H, D_POOL), lambda b, i: (b, jnp.maximum(i * per_tile - 1, 0), 0)),
            pl.BlockSpec((None, H, D_POOL), lambda b, i: (b, jnp.minimum((i + 1) * per_tile, n_halo - 1), 0)),
            y_spec, y_spec,
            mem_spec, mem_spec,
            _const_spec((1, D)),
            _const_spec((D, D_POOL + D_SSM + 2 * D)),
            _const_spec((len(POOL_WINDOWS), POOL_GROUP, POOL_GROUP)),
            _const_spec((1, D_POOL)),
            _const_spec((D_POOL, D)),
            _const_spec((D_SSM, D)),
            _const_spec((D_SSM, D)),
            _const_spec((D, D)),
            _const_spec((1, D)),
            _const_spec((D, D)),
            _const_spec((D, D)),
        ],
        out_specs=tok,
        out_shape=jax.ShapeDtypeStruct((B, L, D), F32),
        compiler_params=pltpu.CompilerParams(
            dimension_semantics=("arbitrary", "arbitrary"), vmem_limit_bytes=V7X_VMEM_LIMIT_BYTES),
        name="mix_attn",
    )(h, p, p, p, yf, yb, k, v, nmix, wgate, poolw, pscale, wpp, wgluv, wglug, wmo, nx, wq, wxo)


def _block_diag_in(w):
    G, P, Hh = w.shape
    per_channel = jnp.transpose(w, (2, 0, 1)).reshape(Hh, G * P)
    same_group = (jnp.arange(G * Hh)[:, None] // Hh) == (jnp.arange(G * P)[None, :] // P)
    return jnp.where(same_group, jnp.tile(per_channel, (G, 1)), 0.0)


def _block_diag_out(w):
    G, Hh, P = w.shape
    per_state = jnp.transpose(w, (0, 2, 1)).reshape(G * P, Hh)
    same_group = (jnp.arange(G * P)[:, None] // P) == (jnp.arange(G * Hh)[None, :] // Hh)
    return jnp.where(same_group, jnp.tile(per_state, (1, G)), 0.0)


def kernel(x, mem, ffn1_norm, ffn1_w_gate, ffn1_w_up, ffn1_w_down, mix_norm, w_in, pool_w, pool_scale,
           w_pool_proj, ssm_a_re, ssm_a_im, ssm_log_dt, ssm_b_re, ssm_b_im, ssm_c_re, ssm_c_im, ssm_d,
           w_glu_val, w_glu_gate, w_mix_out, xattn_norm, mem_norm, w_q, w_kv, w_xo, ffn2_norm,
           ffn2_w_gate, ffn2_w_up, ffn2_w_down, final_norm):
    depth = ffn1_norm.shape[0]
    row = lambda v: v.reshape(1, -1)
    bf = lambda w: w.astype(BF16)
    h = x
    for l in range(depth):
        k_mem, v_mem, cast = _kv_call(
            mem, row(mem_norm[l]), w_kv[l],
            [w_in[l], pool_w[l].reshape(-1, POOL_GROUP), w_pool_proj[l], w_glu_val[l], w_glu_gate[l],
             w_mix_out[l], w_q[l], w_xo[l]])
        w_in_b, pool_w_b, w_pp_b, w_gluv_b, w_glug_b, w_mo_b, w_q_b, w_xo_b = cast

        h, p, s = _ffn1_call(
            h, row(ffn1_norm[l]), ffn1_w_gate[l], ffn1_w_up[l], ffn1_w_down[l], row(mix_norm[l]), w_in[l])

        G, P = N_SSM_GROUPS, SSM_STATE
        per_state = lambda a: a.reshape(2, 1, G * P)
        log_dt = jnp.broadcast_to(ssm_log_dt[l][:, :, None], (2, G, P))
        yf, yb, (ffn2_wg_b, ffn2_wu_b, ffn2_wd_b) = _ssm_call(
            s, per_state(ssm_a_re[l]), per_state(ssm_a_im[l]), per_state(log_dt),
            jax.vmap(_block_diag_in)(ssm_b_re[l]), jax.vmap(_block_diag_in)(ssm_b_im[l]),
            bf(jax.vmap(_block_diag_out)(ssm_c_re[l])), bf(jax.vmap(_block_diag_out)(ssm_c_im[l])),
            row(ssm_d[l]), [ffn2_w_gate[l], ffn2_w_up[l], ffn2_w_down[l]])

        h = _mix_call(
            h, p, yf, yb, k_mem, v_mem,
            row(mix_norm[l]), w_in_b, pool_w_b.reshape(pool_w[l].shape), row(pool_scale[l]),
            w_pp_b, w_gluv_b, w_glug_b, w_mo_b, row(xattn_norm[l]), w_q_b, w_xo_b)

        last = l == depth - 1
        assert last, "deeper stacks need a non-final variant of the second SwiGLU kernel"
        h = _ffn2_call(h, row(ffn2_norm[l]), ffn2_wg_b, ffn2_wu_b, ffn2_wd_b, row(final_norm))
    return h
```

```python
import functools
import math

import jax
import jax.numpy as jnp
from jax import lax
from jax.experimental import pallas as pl
from jax.experimental.pallas import tpu as pltpu

F32 = jnp.float32
BF16 = jnp.bfloat16

D_MODEL = 1024
N_MEM = 256
D_FF = 2816
D_POOL = 512
POOL_WINDOWS = (2, 4, 8, 16)
POOL_GROUP = 128
D_SSM = 256
N_SSM_GROUPS = 16
SSM_STATE = 64
D_STATE = N_SSM_GROUPS * SSM_STATE
N_XHEADS = 4
XHEAD_DIM = 256
EPS = 1e-6

V7X_VMEM_LIMIT_BYTES = 58 * 1024 * 1024
V7X_SUBLANES = 8
V7X_BF16_SUBLANES = 16
POOL_HALO = 16

FFN_TILE = 1024
MIX_TILE = 1024
SSM_TILE = 128
MIX_SKEW = 1
FFN_SKEW = 1
WEIGHT_CHUNKS = 16
WEIGHT_SLOTS = 4
GROUP_ROWS = 256
FF_CHUNKS = ((0, 768), (768, 1536), (1536, 2304), (2304, 2816))
SSM_AHEAD = 2
SSM_SUBTILE = 16
SCAN_LANES = 512


def _rms(v, g):
    r = lax.rsqrt(jnp.mean(v * v, axis=-1, keepdims=True) + EPS)
    return (v * r) * g


def _dot(a, b):
    return jnp.dot(a, b, preferred_element_type=F32)


def _const_spec(shape):
    nd = len(shape)
    return pl.BlockSpec(shape, lambda *_: (0,) * nd)


def _kv_kernel(*refs, n_cast):
    (mem_ref, g_ref, wkv_ref), refs = refs[:3], refs[3:]
    cast_src, refs = refs[:n_cast], refs[n_cast:]
    (k_ref, v_ref), refs = refs[:2], refs[2:]
    cast_dst, (wkv_bf16,) = refs[:n_cast], refs[n_cast:]

    @pl.when(pl.program_id(0) == 0)
    def _():
        wkv_bf16[...] = wkv_ref[...].astype(BF16)

    m = _rms(mem_ref[...], g_ref[...]).astype(BF16)
    kv = _dot(m, wkv_bf16[...])
    k_ref[...] = kv[:, :D_MODEL].astype(BF16)
    v_ref[...] = kv[:, D_MODEL:].astype(BF16)
    _cast_blocks(cast_src, cast_dst)


def _kv_call(mem, mem_norm, w_kv, to_cast):
    B = mem.shape[0]
    per_batch = pl.BlockSpec((None, N_MEM, D_MODEL), lambda b: (b, 0, 0))
    cast_specs = _cast_specs(to_cast, B, lambda b: b)
    outs = pl.pallas_call(
        functools.partial(_kv_kernel, n_cast=len(to_cast)),
        grid=(B,),
        in_specs=[per_batch, _const_spec((1, D_MODEL)), _const_spec((D_MODEL, 2 * D_MODEL))] + cast_specs,
        out_specs=[per_batch, per_batch] + cast_specs,
        out_shape=[jax.ShapeDtypeStruct((B, N_MEM, D_MODEL), BF16)] * 2
                  + [jax.ShapeDtypeStruct(w.shape, BF16) for w in to_cast],
        scratch_shapes=[pltpu.VMEM((D_MODEL, 2 * D_MODEL), BF16)],
        compiler_params=pltpu.CompilerParams(
            dimension_semantics=("arbitrary",), vmem_limit_bytes=V7X_VMEM_LIMIT_BYTES),
        name="kv_proj",
    )(mem, mem_norm, w_kv, *to_cast)
    return outs[0], outs[1], outs[2:]


def _row_groups(n_rows):
    return [slice(r, r + GROUP_ROWS) for r in range(0, n_rows, GROUP_ROWS)]


def _trace_round_robin(chains, skew=0):
    waiting = list(chains)
    n_started = 0
    live = []
    turn = 0
    while waiting or live:
        while waiting and turn >= n_started * skew:
            live.append(waiting.pop(0))
            n_started += 1
        for c in list(live):
            try:
                next(c)
            except StopIteration:
                live.remove(c)
        turn += 1


def _load_weights_as_bf16(jobs, sem):
    chunks = []
    for src, dst, stage in jobs:
        rows = stage.shape[1]
        for r0 in range(0, src.shape[0], rows):
            chunks.append((src.at[pl.ds(r0, rows)], dst.at[pl.ds(r0, rows)], stage))
    ahead = WEIGHT_SLOTS

    def copy(k):
        src, _, stage = chunks[k]
        return pltpu.make_async_copy(src, stage.at[k % WEIGHT_SLOTS], sem.at[k % WEIGHT_SLOTS])

    for k in range(min(ahead, len(chunks))):
        copy(k).start(priority=k % 2)
    for k, (_, dst, stage) in enumerate(chunks):
        copy(k).wait()
        dst[...] = stage[k % WEIGHT_SLOTS].astype(BF16)
        if k + ahead < len(chunks):
            copy(k + ahead).start(priority=(k + ahead) % 2)


def _is_first_step_2d():
    return jnp.logical_and(pl.program_id(0) == 0, pl.program_id(1) == 0)


def _swiglu_residual(x, g, wg_ref, wu_ref, wd_ref):
    u = _rms(x, g).astype(BF16)
    acc = None
    for c0, c1 in FF_CHUNKS:
        gate = _dot(u, wg_ref[:, c0:c1])
        up = _dot(u, wu_ref[:, c0:c1])
        yield
        act = (gate * jax.nn.sigmoid(gate) * up).astype(BF16)
        part = _dot(act, wd_ref[c0:c1, :])
        acc = part if acc is None else acc + part
        yield
    return x + 0.5 * acc


_HBM = pl.BlockSpec(memory_space=pl.ANY)


def _swiglu_weight_scratch(D):
    return [
        pltpu.VMEM((D, D_FF), BF16),
        pltpu.VMEM((D, D_FF), BF16),
        pltpu.VMEM((D_FF, D), BF16),
        pltpu.VMEM((WEIGHT_SLOTS, _cast_block_rows(D, WEIGHT_CHUNKS), D_FF), F32),
        pltpu.VMEM((WEIGHT_SLOTS, _cast_block_rows(D_FF, WEIGHT_CHUNKS), D), F32),
        pltpu.SemaphoreType.DMA((WEIGHT_SLOTS,)),
    ]


def _cast_block_rows(n_rows, n_steps):
    rows = V7X_BF16_SUBLANES
    while n_rows % rows or n_rows // rows > n_steps:
        rows += V7X_BF16_SUBLANES
    return rows


def _cast_specs(to_cast, n_steps, step_of):
    specs = []
    for w in to_cast:
        rows = _cast_block_rows(w.shape[0], n_steps)
        last = w.shape[0] // rows - 1
        specs.append(pl.BlockSpec(
            (rows, w.shape[1]), lambda *idx, last=last: (jnp.minimum(step_of(*idx), last), 0)))
    return specs


def _cast_blocks(srcs, dsts):
    for src, dst in zip(srcs, dsts):
        dst[...] = src[...].astype(BF16)


def _ffn1_kernel(x_ref, n1_ref, wg_hbm, wu_hbm, wd_hbm, n2_ref, win_hbm, h_ref, p_ref, s_ref,
                 wg_ref, wu_ref, wd_ref, stage_up, stage_down, sem, wps_ref):
    @pl.when(_is_first_step_2d())
    def _():
        n_ps = D_POOL + D_SSM
        _load_weights_as_bf16([
            (wg_hbm, wg_ref, stage_up),
            (wu_hbm, wu_ref, stage_up),
            (wd_hbm, wd_ref, stage_down),
            (win_hbm.at[:, pl.ds(0, n_ps)], wps_ref, stage_up.at[:, :, pl.ds(0, n_ps)]),
        ], sem)

    def chain(rows):
        h = yield from _swiglu_residual(x_ref[rows, :], n1_ref[...], wg_ref, wu_ref, wd_ref)
        h_ref[rows, :] = h
        u = _rms(h, n2_ref[...]).astype(BF16)
        ps = _dot(u, wps_ref[...])
        p_ref[rows, :] = ps[:, :D_POOL]
        s_ref[rows, :] = ps[:, D_POOL:]

    _trace_round_robin([chain(rows) for rows in _row_groups(x_ref.shape[0])], skew=FFN_SKEW)


def _ffn1_call(x, n1, wg, wu, wd, n2, w_in):
    B, L, D = x.shape
    T = FFN_TILE
    tok = pl.BlockSpec((None, T, D), lambda b, i: (b, i, 0))
    return pl.pallas_call(
        _ffn1_kernel,
        grid=(B, L // T),
        in_specs=[tok, _const_spec((1, D)), _HBM, _HBM, _HBM, _const_spec((1, D)), _HBM],
        out_specs=[
            tok,
            pl.BlockSpec((None, T, D_POOL), lambda b, i: (b, i, 0)),
            pl.BlockSpec((None, T, D_SSM), lambda b, i: (b, i, 0)),
        ],
        out_shape=[
            jax.ShapeDtypeStruct((B, L, D), F32),
            jax.ShapeDtypeStruct((B, L, D_POOL), F32),
            jax.ShapeDtypeStruct((B, L, D_SSM), F32),
        ],
        scratch_shapes=_swiglu_weight_scratch(D) + [pltpu.VMEM((D, D_POOL + D_SSM), BF16)],
        compiler_params=pltpu.CompilerParams(
            dimension_semantics=("arbitrary", "arbitrary"), vmem_limit_bytes=V7X_VMEM_LIMIT_BYTES),
        name="ffn1",
    )(x, n1, wg, wu, wd, n2, w_in)


def _ffn2_kernel(h_ref, n_ref, wg_ref, wu_ref, wd_ref, fn_ref, o_ref):
    def chain(rows):
        h = yield from _swiglu_residual(h_ref[rows, :], n_ref[...], wg_ref, wu_ref, wd_ref)
        o_ref[rows, :] = _rms(h, fn_ref[...])

    _trace_round_robin([chain(rows) for rows in _row_groups(h_ref.shape[0])], skew=FFN_SKEW)


def _ffn2_call(h, n, wg, wu, wd, fn):
    B, L, D = h.shape
    T = FFN_TILE
    tok = pl.BlockSpec((None, T, D), lambda b, i: (b, i, 0))
    return pl.pallas_call(
        _ffn2_kernel,
        grid=(B, L // T),
        in_specs=[tok, _const_spec((1, D)), _const_spec((D, D_FF)), _const_spec((D, D_FF)),
                  _const_spec((D_FF, D)), _const_spec((1, D))],
        out_specs=tok,
        out_shape=jax.ShapeDtypeStruct((B, L, D), F32),
        compiler_params=pltpu.CompilerParams(
            dimension_semantics=("arbitrary", "arbitrary"), vmem_limit_bytes=V7X_VMEM_LIMIT_BYTES),
        name="ffn2",
    )(h, n, wg, wu, wd, fn)


def _ssm_kernel(*refs, n_cast):
    (sf_ref, sb_ref, are_ref, aim_ref, ldt_ref, bre_ref, bim_ref, crt_ref, cit_ref, d_ref), refs = refs[:10], refs[10:]
    cast_src, refs = refs[:n_cast], refs[n_cast:]
    (yf_ref, yb_ref), refs = refs[:2], refs[2:]
    cast_dst, refs = refs[:n_cast], refs[n_cast:]
    abar_scr, bmat_scr, cmat_scr, state_scr, bu_scr, xb_scr = refs
    i = pl.program_id(0)
    S = D_STATE
    n_b, steps, _ = sf_ref.shape

    @pl.when(i == 0)
    def _():
        for d in range(2):
            ar = are_ref[d]
            ai = aim_ref[d]
            dt = jnp.exp(ldt_ref[d])
            mag = jnp.exp(dt * ar)
            ang = dt * ai
            abr = mag * jnp.cos(ang)
            abi = mag * jnp.sin(ang)
            den = ar * ar + ai * ai
            nr = abr - 1.0
            qr = (nr * ar + abi * ai) / den
            qi = (abi * ar - nr * ai) / den
            abar_scr[d, 0] = jnp.broadcast_to(abr, (n_b, S))
            abar_scr[d, 1] = jnp.broadcast_to(abi, (n_b, S))
            br = bre_ref[d]
            bi = bim_ref[d]
            bmat_scr[d, :, :S] = (qr * br - qi * bi).astype(BF16)
            bmat_scr[d, :, S:] = (qr * bi + qi * br).astype(BF16)
            cmat_scr[d, :S, :] = crt_ref[d]
            cmat_scr[d, S:, :] = -cit_ref[d]
        state_scr[...] = jnp.zeros_like(state_scr)

    sub = SSM_SUBTILE
    n_sub = steps // sub
    halves = [(slice(c0, c0 + SCAN_LANES), slice(S + c0, S + c0 + SCAN_LANES))
              for c0 in range(0, S, SCAN_LANES)]
    def chain(d, s_ref, y_ref):
        order = list(range(n_sub)) if d == 0 else list(range(n_sub - 1, -1, -1))

        def input_map(j):
            s_nat = s_ref[:, j * sub:(j + 1) * sub, :]
            s_tb = jnp.swapaxes(s_nat, 0, 1).reshape(sub * n_b, D_SSM)
            bu_scr[d, j * sub * n_b:(j + 1) * sub * n_b, :] = _dot(s_tb.astype(BF16), bmat_scr[d])

        coef = [(abar_scr[d, 0, :, re], abar_scr[d, 1, :, re]) for re, _ in halves]
        state = [(state_scr[d, 0, :, re], state_scr[d, 1, :, re]) for re, _ in halves]
        for j in order[:SSM_AHEAD]:
            input_map(j)
        yield
        for pos, j in enumerate(order):
            if pos + SSM_AHEAD < n_sub:
                input_map(order[pos + SSM_AHEAD])
            for hf, (re, im) in enumerate(halves):
                ar, ai = coef[hf]
                xr, xi = state[hf]
                for k in range(0, sub, 2):
                    lo = j * sub + (k if d == 0 else sub - 2 - k)
                    got = {}
                    for t in ((lo, lo + 1) if d == 0 else (lo + 1, lo)):
                        rows = slice(n_b * t, n_b * (t + 1))
                        nxr = ar * xr - ai * xi + bu_scr[d, rows, re]
                        nxi = ar * xi + ai * xr + bu_scr[d, rows, im]
                        xr, xi = nxr, nxi
                        got[t] = (nxr, nxi)
                    rows2 = slice(n_b * lo, n_b * (lo + 2))
                    xb_scr[d, rows2, re] = jnp.concatenate([got[lo][0], got[lo + 1][0]], axis=0).astype(BF16)
                    xb_scr[d, rows2, im] = jnp.concatenate([got[lo][1], got[lo + 1][1]], axis=0).astype(BF16)
                state[hf] = (xr, xi)
            rows_j = slice(j * sub * n_b, (j + 1) * sub * n_b)
            y_tb = _dot(xb_scr[d, rows_j, :], cmat_scr[d])
            y = jnp.swapaxes(y_tb.reshape(sub, n_b, D_SSM), 0, 1)
            if d == 0:
                y = y + sf_ref[:, j * sub:(j + 1) * sub, :] * d_ref[...]
            y_ref[:, j * sub:(j + 1) * sub, :] = y
            yield
        for hf, (re, _) in enumerate(halves):
            state_scr[d, 0, :, re] = state[hf][0]
            state_scr[d, 1, :, re] = state[hf][1]

    _trace_round_robin([chain(0, sf_ref, yf_ref), chain(1, sb_ref, yb_ref)])
    _cast_blocks(cast_src, cast_dst)


def _ssm_call(s, a_re, a_im, log_dt, b_re_bd, b_im_bd, c_re_t, c_im_t, d_skip, to_cast):
    B, L, _ = s.shape
    assert B == V7X_SUBLANES, "the scan keeps one batch element per sublane"
    T = SSM_TILE
    n = L // T
    S = D_STATE
    fwd = pl.BlockSpec((B, T, D_SSM), lambda i: (0, i, 0))
    bwd = pl.BlockSpec((B, T, D_SSM), lambda i: (0, n - 1 - i, 0))
    cast_specs = _cast_specs(to_cast, n, lambda i: i)
    outs = pl.pallas_call(
        functools.partial(_ssm_kernel, n_cast=len(to_cast)),
        grid=(n,),
        in_specs=[
            fwd, bwd,
            _const_spec((2, 1, S)), _const_spec((2, 1, S)), _const_spec((2, 1, S)),
            _const_spec((2, D_SSM, S)), _const_spec((2, D_SSM, S)),
            _const_spec((2, S, D_SSM)), _const_spec((2, S, D_SSM)),
            _const_spec((1, D_SSM)),
        ] + cast_specs,
        out_specs=[fwd, bwd] + cast_specs,
        out_shape=[jax.ShapeDtypeStruct((B, L, D_SSM), F32)] * 2
                  + [jax.ShapeDtypeStruct(w.shape, BF16) for w in to_cast],
        scratch_shapes=[
            pltpu.VMEM((2, 2, B, S), F32),
            pltpu.VMEM((2, D_SSM, 2 * S), BF16),
            pltpu.VMEM((2, 2 * S, D_SSM), BF16),
            pltpu.VMEM((2, 2, B, S), F32),
            pltpu.VMEM((2, B * T, 2 * S), F32),
            pltpu.VMEM((2, B * T, 2 * S), BF16),
        ],
        compiler_params=pltpu.CompilerParams(
            dimension_semantics=("arbitrary",), vmem_limit_bytes=V7X_VMEM_LIMIT_BYTES),
        name="ssm_scan",
    )(s, s, a_re, a_im, log_dt, b_re_bd, b_im_bd, c_re_t, c_im_t, d_skip, *to_cast)
    return outs[0], outs[1], outs[2:]


def _window_sum(v, w):
    n = v.shape[0]
    ahead = lambda a, k: pltpu.roll(a, n - k, axis=0)
    behind = lambda a, k: pltpu.roll(a, k, axis=0)
    if w == 2:
        return v + behind(v, 1)
    cur, span = v + ahead(v, 1), 2
    while span < w // 2:
        cur, span = cur + ahead(cur, span), 2 * span
    return cur + behind(cur, span)


def _mix_kernel(h_ref, p_ref, pl_ref, pr_ref, yf_ref, yb_ref, k_ref, v_ref,
                nmix_ref, win_ref, poolw_ref, pscale_ref, wpp_ref, wgluv_ref, wglug_ref, wmo_ref,
                nx_ref, wq_ref, wxo_ref, o_ref, *, seq_len):
    i = pl.program_id(1)
    n_i = pl.num_programs(1)
    T = h_ref.shape[0]
    H = POOL_HALO

    halo_l = jnp.where(i > 0, pl_ref[...], 0.0)
    halo_r = jnp.where(i < n_i - 1, pr_ref[...], 0.0)
    edge = V7X_SUBLANES
    t_edge = lax.broadcasted_iota(jnp.int32, (edge, POOL_GROUP), 0)
    pooled_groups = []
    for g, w in enumerate(POOL_WINDOWS):
        left = w // 2
        right = w - 1 - left
        lanes = slice(g * POOL_GROUP, (g + 1) * POOL_GROUP)
        v = jnp.concatenate([halo_l[:, lanes], p_ref[:, lanes], halo_r[:, lanes]], axis=0)
        tot = _window_sum(v, w)[H:H + T]
        self_ = v[H:H + T]

        def clipped(t0, rows):
            t = i * T + t0 + t_edge
            cnt = jnp.minimum(t + (right + 1), seq_len) - jnp.maximum(t - left, 0)
            return tot[rows] / cnt.astype(F32) - self_[rows]

        pooled_groups.append(jnp.concatenate([
            clipped(0, slice(0, edge)),
            tot[edge:T - edge] * (1.0 / w) - self_[edge:T - edge],
            clipped(T - edge, slice(T - edge, T)),
        ], axis=0).astype(BF16))

    def chain(rows):
        h = h_ref[rows, :]
        mixed = [_dot(pooled_groups[g][rows], poolw_ref[g]) for g in range(len(POOL_WINDOWS))]
        yield
        mixed = jnp.concatenate(mixed, axis=-1) * pscale_ref[...]
        z_pool = _dot(mixed.astype(BF16), wpp_ref[...])
        yield

        y = jax.nn.gelu(yf_ref[rows, :] + yb_ref[rows, :]).astype(BF16)
        glu_val = _dot(y, wgluv_ref[...])
        glu_gate = _dot(y, wglug_ref[...])
        yield
        z_ssm = glu_val * jax.nn.sigmoid(glu_gate)

        u = _rms(h, nmix_ref[...]).astype(BF16)
        gates = _dot(u, win_ref[:, D_POOL + D_SSM:])
        yield
        merged = (jax.nn.sigmoid(gates[:, :D_MODEL]) * z_pool
                  + jax.nn.sigmoid(gates[:, D_MODEL:]) * z_ssm)
        h = h + _dot(merged.astype(BF16), wmo_ref[...])
        yield

        q = _dot(_rms(h, nx_ref[...]).astype(BF16), wq_ref[...])
        yield
        head_cols = [slice(hd * XHEAD_DIM, (hd + 1) * XHEAD_DIM) for hd in range(N_XHEADS)]
        scores = [lax.dot_general(q[:, cols].astype(BF16), k_ref[:, cols], (((1,), (1,)), ((), ())),
                                  preferred_element_type=F32) / math.sqrt(XHEAD_DIM) for cols in head_cols]
        yield
        heads = []
        for sc, cols in zip(scores, head_cols):
            e = jnp.exp(sc - jnp.max(sc, axis=-1, keepdims=True))
            probs = e / jnp.sum(e, axis=-1, keepdims=True)
            heads.append(_dot(probs.astype(BF16), v_ref[:, cols]))
        yield
        o = jnp.concatenate(heads, axis=-1).astype(BF16)
        o_ref[rows, :] = h + _dot(o, wxo_ref[...])

    _trace_round_robin([chain(rows) for rows in _row_groups(T)], skew=MIX_SKEW)


def _mix_call(h, p, yf, yb, k, v, nmix, wgate, poolw, pscale, wpp, wgluv, wglug, wmo, nx, wq, wxo):
    B, L, D = h.shape
    T = MIX_TILE
    H = POOL_HALO
    per_tile = T // H
    n_halo = L // H
    tok = pl.BlockSpec((None, T, D), lambda b, i: (b, i, 0))
    y_spec = pl.BlockSpec((None, T, D_SSM), lambda b, i: (b, i, 0))
    mem_spec = pl.BlockSpec((None, N_MEM, D), lambda b, i: (b, 0, 0))
    return pl.pallas_call(
        functools.partial(_mix_kernel, seq_len=L),
        grid=(B, L // T),
        in_specs=[
            tok,
            pl.BlockSpec((None, T, D_POOL), lambda b, i: (b, i, 0)),
            pl.BlockSpec((None, H, D_POOL), lambda b, i: (b, jnp.maximum(i * per_tile - 1, 0), 0)),
            pl.BlockSpec((None, H, D_POOL), lambda b, i: (b, jnp.minimum((i + 1) * per_tile, n_halo - 1), 0)),
            y_spec, y_spec,
            mem_spec, mem_spec,
            _const_spec((1, D)),
            _const_spec((D, D_POOL + D_SSM + 2 * D)),
            _const_spec((len(POOL_WINDOWS), POOL_GROUP, POOL_GROUP)),
            _const_spec((1, D_POOL)),
            _const_spec((D_POOL, D)),
            _const_spec((D_SSM, D)),
            _const_spec((D_SSM, D)),
            _const_spec((D, D)),
            _const_spec((1, D)),
            _const_spec((D, D)),
            _const_spec((D, D)),
        ],
        out_specs=tok,
        out_shape=jax.ShapeDtypeStruct((B, L, D), F32),
        compiler_params=pltpu.CompilerParams(
            dimension_semantics=("arbitrary", "arbitrary"), vmem_limit_bytes=V7X_VMEM_LIMIT_BYTES),
        name="mix_attn",
    )(h, p, p, p, yf, yb, k, v, nmix, wgate, poolw, pscale, wpp, wgluv, wglug, wmo, nx, wq, wxo)


def _block_diag_in(w):
    G, P, Hh = w.shape
    per_channel = jnp.transpose(w, (2, 0, 1)).reshape(Hh, G * P)
    same_group = (jnp.arange(G * Hh)[:, None] // Hh) == (jnp.arange(G * P)[None, :] // P)
    return jnp.where(same_group, jnp.tile(per_channel, (G, 1)), 0.0)


def _block_diag_out(w):
    G, Hh, P = w.shape
    per_state = jnp.transpose(w, (0, 2, 1)).reshape(G * P, Hh)
    same_group = (jnp.arange(G * P)[:, None] // P) == (jnp.arange(G * Hh)[None, :] // Hh)
    return jnp.where(same_group, jnp.tile(per_state, (1, G)), 0.0)


def kernel(x, mem, ffn1_norm, ffn1_w_gate, ffn1_w_up, ffn1_w_down, mix_norm, w_in, pool_w, pool_scale,
           w_pool_proj, ssm_a_re, ssm_a_im, ssm_log_dt, ssm_b_re, ssm_b_im, ssm_c_re, ssm_c_im, ssm_d,
           w_glu_val, w_glu_gate, w_mix_out, xattn_norm, mem_norm, w_q, w_kv, w_xo, ffn2_norm,
           ffn2_w_gate, ffn2_w_up, ffn2_w_down, final_norm):
    depth = ffn1_norm.shape[0]
    row = lambda v: v.reshape(1, -1)
    bf = lambda w: w.astype(BF16)
    h = x
    for l in range(depth):
        k_mem, v_mem, cast = _kv_call(
            mem, row(mem_norm[l]), w_kv[l],
            [w_in[l], pool_w[l].reshape(-1, POOL_GROUP), w_pool_proj[l], w_glu_val[l], w_glu_gate[l],
             w_mix_out[l], w_q[l], w_xo[l]])
        w_in_b, pool_w_b, w_pp_b, w_gluv_b, w_glug_b, w_mo_b, w_q_b, w_xo_b = cast

        h, p, s = _ffn1_call(
            h, row(ffn1_norm[l]), ffn1_w_gate[l], ffn1_w_up[l], ffn1_w_down[l], row(mix_norm[l]), w_in[l])

        G, P = N_SSM_GROUPS, SSM_STATE
        per_state = lambda a: a.reshape(2, 1, G * P)
        log_dt = jnp.broadcast_to(ssm_log_dt[l][:, :, None], (2, G, P))
        yf, yb, (ffn2_wg_b, ffn2_wu_b, ffn2_wd_b) = _ssm_call(
            s, per_state(ssm_a_re[l]), per_state(ssm_a_im[l]), per_state(log_dt),
            jax.vmap(_block_diag_in)(ssm_b_re[l]), jax.vmap(_block_diag_in)(ssm_b_im[l]),
            bf(jax.vmap(_block_diag_out)(ssm_c_re[l])), bf(jax.vmap(_block_diag_out)(ssm_c_im[l])),
            row(ssm_d[l]), [ffn2_w_gate[l], ffn2_w_up[l], ffn2_w_down[l]])

        h = _mix_call(
            h, p, yf, yb, k_mem, v_mem,
            row(mix_norm[l]), w_in_b, pool_w_b.reshape(pool_w[l].shape), row(pool_scale[l]),
            w_pp_b, w_gluv_b, w_glug_b, w_mo_b, row(xattn_norm[l]), w_q_b, w_xo_b)

        last = l == depth - 1
        assert last, "deeper stacks need a non-final variant of the second SwiGLU kernel"
        h = _ffn2_call(h, row(ffn2_norm[l]), ffn2_wg_b, ffn2_wu_b, ffn2_wd_b, row(final_norm))
    return h
```
